```python
import math
import jax, jax.numpy as jnp
from jax import lax
import numpy as np

D_MODEL = 2048
BATCH = 2
SEQ = 4096
DEPTH = 4

N_A_LAYERS = DEPTH // 2
N_B_LAYERS = DEPTH - N_A_LAYERS
D_RNN = D_MODEL
N_RNN_BLOCKS = 8
RNN_BLOCK = D_RNN // N_RNN_BLOCKS
CONV_WIDTH = 4
LRU_C = 8.0
N_HEADS = 16
HEAD_DIM = D_MODEL // N_HEADS
D_ATTN = N_HEADS * HEAD_DIM
MOBA_BLOCK = 256
MOBA_TOPK = 3
Q_CHUNK = 32
ROPE_THETA = 10000.0
N_EXPERTS = 64
TOP_K = 8
N_GROUPS = 8
TOPK_GROUPS = 4
D_EXPERT = 512
ROUTED_SCALE = 2.5
ROW_BLOCK = 128
DN_ALPHA = (2.0 * DEPTH) ** 0.25
DN_BETA = (8.0 * DEPTH) ** -0.25
LN_EPS = 1e-5
NEG = -1e30

kernel_name = "hawk_yoco_moba_moe_deepnorm"


def layer_norm(x, g, b):
    xf = x.astype(jnp.float32)
    mu = xf.mean(-1, keepdims=True)
    var = jnp.square(xf - mu).mean(-1, keepdims=True)
    return ((xf - mu) * lax.rsqrt(var + LN_EPS) * g.astype(jnp.float32) + b.astype(jnp.float32)).astype(x.dtype)


def rope_tables(seq):
    inv = ROPE_THETA ** (-jnp.arange(0, HEAD_DIM, 2, dtype=jnp.float32) / HEAD_DIM)
    ang = jnp.arange(seq, dtype=jnp.float32)[:, None] * inv[None, :]
    ang = jnp.concatenate([ang, ang], axis=-1)
    return jnp.cos(ang), jnp.sin(ang)


def apply_rope(x, cos, sin):
    x1, x2 = jnp.split(x, 2, axis=-1)
    rot = jnp.concatenate([-x2, x1], axis=-1)
    return (x.astype(jnp.float32) * cos + rot.astype(jnp.float32) * sin).astype(x.dtype)


def _lin_combine(c1, c2):
    a1, b1 = c1
    a2, b2 = c2
    return a1 * a2, a2 * b1 + b2


def rglru_block(x, w_in, conv_w, conv_b, gate_w, gate_b, lam, w_out):
    B_, S_, _ = x.shape
    proj = jnp.einsum('bsd,de->bse', x, w_in)
    gate_branch, u = jnp.split(proj, 2, axis=-1)
    u_pad = jnp.pad(u, ((0, 0), (CONV_WIDTH - 1, 0), (0, 0)))
    u = conv_b + sum(u_pad[:, k:k + S_] * conv_w[k] for k in range(CONV_WIDTH))
    ub = u.reshape(B_, S_, N_RNN_BLOCKS, RNN_BLOCK)
    gates = jax.nn.sigmoid((jnp.einsum('bsnc,nce->bsne', ub, gate_w) + gate_b).astype(jnp.float32))
    r = gates[..., :RNN_BLOCK].reshape(B_, S_, D_RNN)
    i = gates[..., RNN_BLOCK:].reshape(B_, S_, D_RNN)
    log_a = -LRU_C * r * jax.nn.softplus(-lam.astype(jnp.float32))
    a = jnp.exp(log_a)
    mult = jnp.sqrt(-jnp.expm1(2.0 * log_a))
    bterm = mult * i * u.astype(jnp.float32)
    _, h = lax.associative_scan(_lin_combine, (a, bterm), axis=1)
    y = jax.nn.gelu(gate_branch, approximate=True) * h.astype(x.dtype)
    return jnp.einsum('bse,ed->bsd', y, w_out)


def shared_kv(x, kv_w, cos, sin):
    B_, S_, _ = x.shape
    kv = jnp.einsum('bsd,de->bse', x, kv_w).reshape(B_, S_, 2, N_HEADS, HEAD_DIM)
    k = apply_rope(kv[:, :, 0].transpose(0, 2, 1, 3), cos, sin)
    v = kv[:, :, 1].transpose(0, 2, 1, 3)
    n_blk = -(-S_ // MOBA_BLOCK)
    pad = n_blk * MOBA_BLOCK - S_
    k = jnp.pad(k, ((0, 0), (0, 0), (0, pad), (0, 0)))
    v = jnp.pad(v, ((0, 0), (0, 0), (0, pad), (0, 0)))
    k_blocks = k.reshape(B_, N_HEADS, n_blk, MOBA_BLOCK, HEAD_DIM)
    v_blocks = v.reshape(B_, N_HEADS, n_blk, MOBA_BLOCK, HEAD_DIM)
    k_mean = k_blocks.astype(jnp.float32).mean(axis=3)
    return k_blocks, v_blocks, k_mean


def moba_attention(q, k_blocks, v_blocks, k_mean):
    B_, H_, S_, hd = q.shape
    n_blk = k_blocks.shape[2]
    n_sel = min(MOBA_TOPK, n_blk)
    n_chunks = S_ // Q_CHUNK
    scale = HEAD_DIM ** -0.5
    k_flat = k_blocks.reshape(B_, H_, n_blk * MOBA_BLOCK, hd)
    v_flat = v_blocks.reshape(B_, H_, n_blk * MOBA_BLOCK, hd)
    q_chunks = q.reshape(B_, H_, n_chunks, Q_CHUNK, hd).transpose(2, 0, 1, 3, 4)
    bidx = jnp.arange(B_)[:, None, None, None]
    hidx = jnp.arange(H_)[None, :, None, None]

    def one_chunk(args):
        qc, c = args
        t0 = c * Q_CHUNK
        pos = t0 + jnp.arange(Q_CHUNK)
        own = t0 // MOBA_BLOCK
        gate = jnp.einsum('bhqd,bhnd->bhqn', qc.astype(jnp.float32), k_mean)
        gate = jnp.where(jnp.arange(n_blk) < own, gate, -jnp.inf)
        _, sel = lax.top_k(gate, n_sel)
        sel_valid = sel < own
        k_sel = k_blocks[bidx, hidx, sel]
        v_sel = v_blocks[bidx, hidx, sel]
        s_sel = jnp.einsum('bhqd,bhqnld->bhqnl', qc, k_sel).astype(jnp.float32) * scale
        s_sel = jnp.where(sel_valid[..., None], s_sel, NEG)
        k_own = lax.dynamic_slice_in_dim(k_flat, own * MOBA_BLOCK, MOBA_BLOCK, axis=2)
        v_own = lax.dynamic_slice_in_dim(v_flat, own * MOBA_BLOCK, MOBA_BLOCK, axis=2)
        s_own = jnp.einsum('bhqd,bhld->bhql', qc, k_own).astype(jnp.float32) * scale
        key_pos = own * MOBA_BLOCK + jnp.arange(MOBA_BLOCK)
        s_own = jnp.where(key_pos[None, :] <= pos[:, None], s_own, NEG)
        scores = jnp.concatenate([s_sel.reshape(B_, H_, Q_CHUNK, n_sel * MOBA_BLOCK), s_own], axis=-1)
        p = jax.nn.softmax(scores, axis=-1)
        p_sel = p[..., :n_sel * MOBA_BLOCK].reshape(B_, H_, Q_CHUNK, n_sel, MOBA_BLOCK).astype(v_sel.dtype)
        p_own = p[..., n_sel * MOBA_BLOCK:].astype(v_own.dtype)
        return (jnp.einsum('bhqnl,bhqnld->bhqd', p_sel, v_sel)
                + jnp.einsum('bhql,bhld->bhqd', p_own, v_own))

    out = lax.map(one_chunk, (q_chunks, jnp.arange(n_chunks)))
    return out.transpose(1, 2, 0, 3, 4).reshape(B_, H_, S_, hd)


def moba_block(x, w_q, w_o, k_blocks, v_blocks, k_mean, cos, sin):
    B_, S_, _ = x.shape
    q = jnp.einsum('bsd,de->bse', x, w_q).reshape(B_, S_, N_HEADS, HEAD_DIM).transpose(0, 2, 1, 3)
    q = apply_rope(q, cos, sin)
    o = moba_attention(q, k_blocks, v_blocks, k_mean)
    o = o.transpose(0, 2, 1, 3).reshape(B_, S_, D_ATTN)
    return jnp.einsum('bse,ed->bsd', o, w_o)


def swiglu(x, w_gu, w_down):
    g, u = jnp.split(x @ w_gu, 2, axis=-1)
    return (jax.nn.silu(g) * u) @ w_down


def moe_ffn(x, router_w, router_bias, w_gu, w_down, sh_gu, sh_down):
    B_, S_, D = x.shape
    N = B_ * S_
    xf = x.reshape(N, D)
    scores = jax.nn.sigmoid((xf @ router_w).astype(jnp.float32))
    sel_scores = scores + router_bias.astype(jnp.float32)
    grp = sel_scores.reshape(N, N_GROUPS, N_EXPERTS // N_GROUPS)
    grp_score = lax.top_k(grp, 2)[0].sum(-1)
    _, top_g = lax.top_k(grp_score, TOPK_GROUPS)
    g_mask = jax.nn.one_hot(top_g, N_GROUPS, dtype=jnp.float32).sum(-2) > 0
    e_mask = jnp.repeat(g_mask, N_EXPERTS // N_GROUPS, axis=-1)
    _, top_e = lax.top_k(jnp.where(e_mask, sel_scores, -jnp.inf), TOP_K)
    w = jnp.take_along_axis(scores, top_e, axis=-1)
    w = w / w.sum(-1, keepdims=True) * ROUTED_SCALE
    A = N * TOP_K
    flat_e = top_e.reshape(A)
    flat_tok = jnp.repeat(jnp.arange(N, dtype=jnp.int32), TOP_K)
    flat_w = w.reshape(A)
    order = jnp.argsort(flat_e)
    e_sorted, tok_sorted, w_sorted = flat_e[order], flat_tok[order], flat_w[order]
    counts = jnp.zeros((N_EXPERTS,), jnp.int32).at[flat_e].add(1)
    padded = (counts + ROW_BLOCK - 1) // ROW_BLOCK * ROW_BLOCK
    start = jnp.cumsum(counts) - counts
    pend = jnp.cumsum(padded)
    pstart = pend - padded
    dest = pstart[e_sorted] + (jnp.arange(A, dtype=jnp.int32) - start[e_sorted])
    n_blocks = (A + N_EXPERTS * (ROW_BLOCK - 1) + ROW_BLOCK - 1) // ROW_BLOCK
    P = n_blocks * ROW_BLOCK
    row_tok = jnp.full((P,), N, jnp.int32).at[dest].set(tok_sorted)
    row_w = jnp.zeros((P,), jnp.float32).at[dest].set(w_sorted)
    block_e = jnp.minimum(jnp.searchsorted(pend, jnp.arange(n_blocks, dtype=jnp.int32) * ROW_BLOCK, side='right'), N_EXPERTS - 1)
    x_rows = jnp.concatenate([xf, jnp.zeros((1, D), xf.dtype)], axis=0)[row_tok].reshape(n_blocks, ROW_BLOCK, D)

    def expert_block(args):
        xb, e = args
        return swiglu(xb, w_gu[e], w_down[e])

    y_rows = lax.map(expert_block, (x_rows, block_e)).reshape(P, D)
    routed = jax.ops.segment_sum(y_rows.astype(jnp.float32) * row_w[:, None], row_tok, num_segments=N + 1)[:N]
    shared = swiglu(xf, sh_gu, sh_down)
    return (routed.astype(xf.dtype) + shared).reshape(B_, S_, D)


def setup_inputs(seed: int = 0) -> dict:
    key = jax.random.key(seed)
    ks = jax.random.split(key, 24)
    f32 = jnp.float32
    nrm = lambda k, shape, s: jax.random.normal(k, shape, f32) * s
    x = jax.random.normal(ks[0], (BATCH, SEQ, D_MODEL), f32)
    ln_mix_g = 1.0 + nrm(ks[1], (DEPTH, D_MODEL), 0.01)
    ln_mix_b = nrm(ks[2], (DEPTH, D_MODEL), 0.01)
    ln_ffn_g = 1.0 + nrm(ks[3], (DEPTH, D_MODEL), 0.01)
    ln_ffn_b = nrm(ks[4], (DEPTH, D_MODEL), 0.01)
    a_w_in = nrm(ks[5], (N_A_LAYERS, D_MODEL, 2 * D_RNN), D_MODEL ** -0.5)
    a_conv_w = nrm(ks[6], (N_A_LAYERS, CONV_WIDTH, D_RNN), CONV_WIDTH ** -0.5)
    a_conv_b = nrm(ks[7], (N_A_LAYERS, D_RNN), 0.01)
    a_gate_w = nrm(ks[8], (N_A_LAYERS, N_RNN_BLOCKS, RNN_BLOCK, 2 * RNN_BLOCK), RNN_BLOCK ** -0.5)
    a_gate_b = nrm(ks[9], (N_A_LAYERS, N_RNN_BLOCKS, 2 * RNN_BLOCK), 0.01)
    u = jax.random.uniform(ks[10], (N_A_LAYERS, D_RNN), f32, 0.9, 0.999)
    a0 = u ** (1.0 / LRU_C)
    a_lambda = jnp.log(a0) - jnp.log1p(-a0)
    a_w_out = nrm(ks[11], (N_A_LAYERS, D_RNN, D_MODEL), D_RNN ** -0.5 * DN_BETA)
    kv_w = jnp.concatenate([nrm(ks[12], (D_MODEL, D_ATTN), D_MODEL ** -0.5),
                            nrm(ks[13], (D_MODEL, D_ATTN), D_MODEL ** -0.5 * DN_BETA)], axis=1)
    b_w_q = nrm(ks[14], (N_B_LAYERS, D_MODEL, D_ATTN), D_MODEL ** -0.5)
    b_w_o = nrm(ks[15], (N_B_LAYERS, D_ATTN, D_MODEL), D_ATTN ** -0.5 * DN_BETA)
    router_w = nrm(ks[16], (DEPTH, D_MODEL, N_EXPERTS), D_MODEL ** -0.5)
    router_bias = nrm(ks[17], (DEPTH, N_EXPERTS), 0.01)
    moe_w_gu = nrm(ks[18], (DEPTH, N_EXPERTS, D_MODEL, 2 * D_EXPERT), D_MODEL ** -0.5)
    moe_w_down = nrm(ks[19], (DEPTH, N_EXPERTS, D_EXPERT, D_MODEL), D_EXPERT ** -0.5 * DN_BETA)
    sh_w_gu = nrm(ks[20], (DEPTH, D_MODEL, 2 * D_EXPERT), D_MODEL ** -0.5)
    sh_w_down = nrm(ks[21], (DEPTH, D_EXPERT, D_MODEL), D_EXPERT ** -0.5 * DN_BETA)
    return {"x": x, "ln_mix_g": ln_mix_g, "ln_mix_b": ln_mix_b, "ln_ffn_g": ln_ffn_g, "ln_ffn_b": ln_ffn_b,
            "a_w_in": a_w_in, "a_conv_w": a_conv_w, "a_conv_b": a_conv_b, "a_gate_w": a_gate_w,
            "a_gate_b": a_gate_b, "a_lambda": a_lambda, "a_w_out": a_w_out, "kv_w": kv_w,
            "b_w_q": b_w_q, "b_w_o": b_w_o, "router_w": router_w, "router_bias": router_bias,
            "moe_w_gu": moe_w_gu, "moe_w_down": moe_w_down, "sh_w_gu": sh_w_gu, "sh_w_down": sh_w_down}


def reference(x, ln_mix_g, ln_mix_b, ln_ffn_g, ln_ffn_b, a_w_in, a_conv_w, a_conv_b, a_gate_w,
              a_gate_b, a_lambda, a_w_out, kv_w, b_w_q, b_w_o, router_w, router_bias,
              moe_w_gu, moe_w_down, sh_w_gu, sh_w_down):
    S_ = x.shape[1]
    cos, sin = rope_tables(S_)
    k_blocks = v_blocks = k_mean = None
    for layer in range(DEPTH):
        if layer < N_A_LAYERS:
            mix = rglru_block(x, a_w_in[layer], a_conv_w[layer], a_conv_b[layer], a_gate_w[layer],
                              a_gate_b[layer], a_lambda[layer], a_w_out[layer])
        else:
            if layer == N_A_LAYERS:
                k_blocks, v_blocks, k_mean = shared_kv(x, kv_w, cos, sin)
            j = layer - N_A_LAYERS
            mix = moba_block(x, b_w_q[j], b_w_o[j], k_blocks, v_blocks, k_mean, cos, sin)
        x = layer_norm(DN_ALPHA * x + mix, ln_mix_g[layer], ln_mix_b[layer])
        ffn = moe_ffn(x, router_w[layer], router_bias[layer], moe_w_gu[layer], moe_w_down[layer],
                      sh_w_gu[layer], sh_w_down[layer])
        x = layer_norm(DN_ALPHA * x + ffn, ln_ffn_g[layer], ln_ffn_b[layer])
    return x
```

```python
import functools
import math

import jax
import jax.numpy as jnp
from jax import lax
from jax.experimental import pallas as pl
from jax.experimental.pallas import tpu as pltpu

HEAD_DIM = 128
MOBA_BLOCK = 256
MOBA_TOPK = 3
ROPE_THETA = 10000.0
CONV_WIDTH = 4
LRU_C = 8.0
TOP_K = 8
N_GROUPS = 8
TOPK_GROUPS = 4
ROUTED_SCALE = 2.5
LN_EPS = 1e-5
NEG = -1e30

VMEM_LIMIT_BYTES = 56 * 1024 * 1024
SUBLANES = 8
EXPERT_ROW_BLOCK = 256

F32 = jnp.float32
BF16 = jnp.bfloat16


def _params(*sem):
    return pltpu.CompilerParams(dimension_semantics=sem, vmem_limit_bytes=VMEM_LIMIT_BYTES)


def _tile(n, pref):
    t = min(n, pref)
    assert n % t == 0, (n, t)
    return t


def _proj_kernel(x_ref, w_ref, o_ref):
    o_ref[...] = jnp.dot(x_ref[...], w_ref[...], preferred_element_type=F32).astype(o_ref.dtype)


def _proj(x, w, out_dtype):
    m, k = x.shape
    n = w.shape[1]
    tm, tn = _tile(m, 1024), _tile(n, 1024)
    return pl.pallas_call(
        _proj_kernel,
        grid=(m // tm, n // tn),
        in_specs=[pl.BlockSpec((tm, k), lambda i, j: (i, 0)),
                  pl.BlockSpec((k, tn), lambda i, j: (0, j))],
        out_specs=pl.BlockSpec((tm, tn), lambda i, j: (i, j)),
        out_shape=jax.ShapeDtypeStruct((m, n), out_dtype),
        compiler_params=_params("parallel", "arbitrary"),
        name="proj",
    )(x, w)


def _rope_proj_kernel(x_ref, w_ref, cos_ref, sin_ref, o_ref, km_ref, *, n_rope_tiles, tm, tn):
    acc = jnp.dot(x_ref[...], w_ref[...], preferred_element_type=F32)
    j = pl.program_id(1)
    nb = tm // MOBA_BLOCK

    def block_means(val, sl):
        for r in range(nb):
            km_ref[r, :, sl] = jnp.mean(val[r * MOBA_BLOCK:(r + 1) * MOBA_BLOCK], axis=0, keepdims=True)

    @pl.when(j < n_rope_tiles)
    def _():
        cos = cos_ref[...]
        sin = sin_ref[...]
        for h in range(tn // HEAD_DIM):
            sl = slice(h * HEAD_DIM, (h + 1) * HEAD_DIM)
            seg = acc[:, sl]
            roped = seg * cos + pltpu.roll(seg, HEAD_DIM // 2, axis=1) * sin
            o_ref[:, sl] = roped.astype(o_ref.dtype)
            block_means(roped, sl)

    @pl.when(j >= n_rope_tiles)
    def _():
        o_ref[...] = acc.astype(o_ref.dtype)
        block_means(acc, slice(None))


def _rope_proj(x, w, cos, sin_signed, n_rope, seq):
    m, k = x.shape
    n = w.shape[1]
    tm = _tile(seq, 1024)
    tn = _tile(n_rope, 1024)
    assert n % tn == 0 and tm % MOBA_BLOCK == 0
    n_seq_tiles = seq // tm
    kern = functools.partial(_rope_proj_kernel, n_rope_tiles=n_rope // tn, tm=tm, tn=tn)
    return pl.pallas_call(
        kern,
        grid=(m // tm, n // tn),
        in_specs=[pl.BlockSpec((tm, k), lambda i, j: (i, 0)),
                  pl.BlockSpec((k, tn), lambda i, j: (0, j)),
                  pl.BlockSpec((tm, HEAD_DIM), lambda i, j: (i % n_seq_tiles, 0)),
                  pl.BlockSpec((tm, HEAD_DIM), lambda i, j: (i % n_seq_tiles, 0))],
        out_specs=[pl.BlockSpec((tm, tn), lambda i, j: (i, j)),
                   pl.BlockSpec((tm // MOBA_BLOCK, 1, tn), lambda i, j: (i, 0, j))],
        out_shape=[jax.ShapeDtypeStruct((m, n), BF16),
                   jax.ShapeDtypeStruct((m // MOBA_BLOCK, 1, n), F32)],
        compiler_params=_params("parallel", "arbitrary"),
        name="rope_proj",
    )(x, w, cos, sin_signed)


def _layer_norm_store(z, g_ref, b_ref, of_ref, ob_ref):
    mu = jnp.mean(z, axis=-1, keepdims=True)
    zc = z - mu
    var = jnp.mean(zc * zc, axis=-1, keepdims=True)
    out = zc * lax.rsqrt(var + LN_EPS) * g_ref[...] + b_ref[...]
    of_ref[...] = out
    ob_ref[...] = out.astype(BF16)


def _mm_res_ln_kernel(y_ref, w_ref, x_ref, g_ref, b_ref, of_ref, ob_ref, *, alpha):
    mix = jnp.dot(y_ref[...], w_ref[...], preferred_element_type=F32)
    _layer_norm_store(alpha * x_ref[...] + mix, g_ref, b_ref, of_ref, ob_ref)


def _mm_res_ln(y, w, x, g, b, alpha):
    m, k = y.shape
    d = w.shape[1]
    tm = _tile(m, 256)
    row = lambda i: (i, 0)
    const = lambda i: (0, 0)
    return pl.pallas_call(
        functools.partial(_mm_res_ln_kernel, alpha=alpha),
        grid=(m // tm,),
        in_specs=[pl.BlockSpec((tm, k), row), pl.BlockSpec((k, d), const), pl.BlockSpec((tm, d), row),
                  pl.BlockSpec((1, d), const), pl.BlockSpec((1, d), const)],
        out_specs=[pl.BlockSpec((tm, d), row), pl.BlockSpec((tm, d), row)],
        out_shape=[jax.ShapeDtypeStruct((m, d), F32), jax.ShapeDtypeStruct((m, d), BF16)],
        compiler_params=_params("parallel"),
        name="mm_res_ln",
    )(y, w, x, g.reshape(1, d), b.reshape(1, d))


def _rglru_kernel(gb_ref, u_ref, cw_ref, cb_ref, gw_ref, gbias_ref, sp_ref, y_ref,
                  uext_ref, h_ref, a_ref, b_ref, *, ts, n_blocks, cblk):
    t = pl.program_id(1)

    @pl.when(t == 0)
    def _():
        uext_ref[0:SUBLANES, :] = jnp.zeros((SUBLANES, uext_ref.shape[1]), F32)
        h_ref[...] = jnp.zeros_like(h_ref)

    uext_ref[SUBLANES:SUBLANES + ts, :] = u_ref[...]
    u = cb_ref[...] + sum(
        uext_ref[pl.ds(SUBLANES - (CONV_WIDTH - 1) + k, ts), :] * cw_ref[k:k + 1, :]
        for k in range(CONV_WIDTH))
    uext_ref[0:SUBLANES, :] = uext_ref[ts:ts + SUBLANES, :]

    for n in range(n_blocks):
        sl = slice(n * cblk, (n + 1) * cblk)
        un = u[:, sl]
        gates = jnp.dot(un.astype(BF16), gw_ref[n], preferred_element_type=F32) + gbias_ref[n]
        gates = jax.nn.sigmoid(gates)
        r = gates[:, :cblk]
        i_gate = gates[:, cblk:]
        log_a = -LRU_C * r * sp_ref[:, sl]
        a = jnp.exp(log_a)
        mult = jnp.sqrt(1.0 - jnp.exp(2.0 * log_a))
        a_ref[:, sl] = a
        b_ref[:, sl] = mult * i_gate * un

    width = a_ref.shape[1]
    row = lax.broadcasted_iota(jnp.int32, (SUBLANES, width), 0)

    def slab(s, h):
        off = pl.multiple_of(s * SUBLANES, SUBLANES)
        a = a_ref[pl.ds(off, SUBLANES), :]
        b = b_ref[pl.ds(off, SUBLANES), :]
        for sh in (1, 2, 4):
            keep = row >= sh
            a_prev = jnp.where(keep, pltpu.roll(a, sh, axis=0), 1.0)
            b_prev = jnp.where(keep, pltpu.roll(b, sh, axis=0), 0.0)
            b = a * b_prev + b
            a = a * a_prev
        hs = a * h + b
        b_ref[pl.ds(off, SUBLANES), :] = hs
        return hs[SUBLANES - 1:SUBLANES, :]

    h_ref[...] = lax.fori_loop(0, ts // SUBLANES, slab, h_ref[...])
    y_ref[...] = (jax.nn.gelu(gb_ref[...], approximate=True) * b_ref[...]).astype(y_ref.dtype)


def _rglru(proj, conv_w, conv_b, gate_w, gate_b, softplus_neg_lam, batch, seq):
    m, two_d = proj.shape
    d = two_d // 2
    n_blocks, cblk = gate_w.shape[0], gate_w.shape[1]
    ts = _tile(seq, 256)
    nt = seq // ts
    kern = functools.partial(_rglru_kernel, ts=ts, n_blocks=n_blocks, cblk=cblk)
    const2 = lambda b, t: (0, 0)
    const3 = lambda b, t: (0, 0, 0)
    return pl.pallas_call(
        kern,
        grid=(batch, nt),
        in_specs=[pl.BlockSpec((ts, d), lambda b, t: (b * nt + t, 0)),
                  pl.BlockSpec((ts, d), lambda b, t: (b * nt + t, 1)),
                  pl.BlockSpec((CONV_WIDTH, d), const2),
                  pl.BlockSpec((1, d), const2),
                  pl.BlockSpec((n_blocks, cblk, 2 * cblk), const3),
                  pl.BlockSpec((n_blocks, 1, 2 * cblk), const3),
                  pl.BlockSpec((1, d), const2)],
        out_specs=pl.BlockSpec((ts, d), lambda b, t: (b * nt + t, 0)),
        out_shape=jax.ShapeDtypeStruct((m, d), BF16),
        scratch_shapes=[pltpu.VMEM((ts + SUBLANES, d), F32), pltpu.VMEM((1, d), F32),
                        pltpu.VMEM((ts, d), F32), pltpu.VMEM((ts, d), F32)],
        compiler_params=_params("arbitrary", "arbitrary"),
        name="rglru",
    )(proj, proj, conv_w, conv_b.reshape(1, d), gate_w, gate_b.reshape(n_blocks, 1, 2 * cblk),
      softplus_neg_lam.reshape(1, d))


def _moba_kernel(q_ref, k_ref, v_ref, km_ref, o_ref, *, heads, n_kv_blocks, scale):
    i = pl.program_id(2)
    L = MOBA_BLOCK
    nt_dims = (((1,), (1,)), ((), ()))
    blk = lax.broadcasted_iota(jnp.int32, (L, n_kv_blocks), 1)
    row = lax.broadcasted_iota(jnp.int32, (L, L), 0)
    col = lax.broadcasted_iota(jnp.int32, (L, L), 1)

    for h in range(heads):
        sl = slice(h * HEAD_DIM, (h + 1) * HEAD_DIM)
        q = q_ref[:, sl]
        gate = lax.dot_general(q.astype(F32), km_ref[0, :, sl], nt_dims,
                               precision=lax.Precision.HIGHEST, preferred_element_type=F32)
        gate = jnp.where(blk < i, gate, -jnp.inf)
        chosen = jnp.zeros((L, n_kv_blocks), F32)
        for _ in range(MOBA_TOPK):
            best = jnp.max(gate, axis=1, keepdims=True)
            first = jnp.min(jnp.where(gate == best, blk, n_kv_blocks), axis=1, keepdims=True)
            pick = (blk == first) & (best > -jnp.inf)
            chosen = jnp.where(pick, 1.0, chosen)
            gate = jnp.where(pick, -jnp.inf, gate)
        bits = jnp.sum(chosen * jnp.left_shift(1, blk).astype(F32), axis=1, keepdims=True).astype(jnp.int32)

        def step(kj, vj, mask, carry):
            m, l, acc = carry
            s = lax.dot_general(q, kj, nt_dims, preferred_element_type=F32) * scale
            s = jnp.where(mask, s, NEG)
            m_new = jnp.maximum(m, jnp.max(s, axis=1, keepdims=True))
            alpha = jnp.exp(m - m_new)
            p = jnp.exp(s - m_new)
            l = alpha * l + jnp.sum(p, axis=1, keepdims=True)
            acc = alpha * acc + jnp.dot(p.astype(BF16), vj, preferred_element_type=F32)
            return m_new, l, acc

        def past(j, carry):
            off = pl.multiple_of(j * L, L)
            mask = (jnp.right_shift(bits, j) & 1) > 0
            return step(k_ref[pl.ds(off, L), sl], v_ref[pl.ds(off, L), sl], mask, carry)

        init = (jnp.full((L, 1), NEG, F32), jnp.zeros((L, 1), F32), jnp.zeros((L, HEAD_DIM), F32))
        carry = lax.fori_loop(0, i, past, init)
        off = pl.multiple_of(i * L, L)
        _, l, acc = step(k_ref[pl.ds(off, L), sl], v_ref[pl.ds(off, L), sl], col <= row, carry)
        o_ref[:, sl] = (acc / l).astype(o_ref.dtype)


def _moba(q, kv, k_mean, batch, seq):
    m, d_attn = q.shape
    heads = min(4, d_attn // HEAD_DIM)
    gw = heads * HEAD_DIM
    n_groups = d_attn // gw
    nq = seq // MOBA_BLOCK
    kern = functools.partial(_moba_kernel, heads=heads, n_kv_blocks=nq, scale=HEAD_DIM ** -0.5)
    return pl.pallas_call(
        kern,
        grid=(batch, n_groups, nq),
        in_specs=[pl.BlockSpec((MOBA_BLOCK, gw), lambda b, g, i: (b * nq + i, g)),
                  pl.BlockSpec((seq, gw), lambda b, g, i: (b, g)),
                  pl.BlockSpec((seq, gw), lambda b, g, i: (b, n_groups + g)),
                  pl.BlockSpec((1, nq, gw), lambda b, g, i: (b, 0, g))],
        out_specs=pl.BlockSpec((MOBA_BLOCK, gw), lambda b, g, i: (b * nq + i, g)),
        out_shape=jax.ShapeDtypeStruct((m, d_attn), BF16),
        compiler_params=_params("parallel", "parallel", "arbitrary"),
        name="moba",
    )(q, kv, kv, k_mean)


def _router_kernel(x_ref, w_ref, s_ref):
    logits = jnp.dot(x_ref[...], w_ref[...], precision=lax.Precision.HIGHEST, preferred_element_type=F32)
    s_ref[...] = jax.nn.sigmoid(logits)


def _router(x, w):
    m, d = x.shape
    e = w.shape[1]
    tm = _tile(m, 512)
    return pl.pallas_call(
        _router_kernel,
        grid=(m // tm,),
        in_specs=[pl.BlockSpec((tm, d), lambda i: (i, 0)), pl.BlockSpec((d, e), lambda i: (0, 0))],
        out_specs=pl.BlockSpec((tm, e), lambda i: (i, 0)),
        out_shape=jax.ShapeDtypeStruct((m, e), F32),
        compiler_params=_params("parallel"),
        name="router",
    )(x, w)


def _experts_kernel(be_ref, nv_ref, x_ref, rw_ref, wgu_ref, wdn_ref, o_ref, wgu_bf, wdn_bf, *, d_expert):
    i = pl.program_id(0)
    prev = be_ref[jnp.maximum(i - 1, 0)]

    @pl.when((i == 0) | (be_ref[i] != prev))
    def _():
        wgu_bf[...] = wgu_ref[0].astype(BF16)
        wdn_bf[...] = wdn_ref[0].astype(BF16)

    @pl.when(i < nv_ref[0])
    def _():
        h = jnp.dot(x_ref[...], wgu_bf[...], preferred_element_type=F32)
        act = jax.nn.silu(h[:, :d_expert]) * h[:, d_expert:]
        y = jnp.dot(act.astype(BF16), wdn_bf[...], preferred_element_type=F32)
        o_ref[...] = y * rw_ref[...]


def _experts(x_rows, row_w, block_e, n_valid, w_gu, w_down):
    p, d = x_rows.shape
    d_expert = w_down.shape[1]
    tm = EXPERT_ROW_BLOCK
    n_blocks = p // tm
    live = lambda i, be, nv: (jnp.minimum(i, nv[0] - 1), 0)
    grid_spec = pltpu.PrefetchScalarGridSpec(
        num_scalar_prefetch=2,
        grid=(n_blocks,),
        in_specs=[pl.BlockSpec((tm, d), live),
                  pl.BlockSpec((tm, 1), live),
                  pl.BlockSpec((1, d, 2 * d_expert), lambda i, be, nv: (be[i], 0, 0)),
                  pl.BlockSpec((1, d_expert, d), lambda i, be, nv: (be[i], 0, 0))],
        out_specs=pl.BlockSpec((tm, d), live),
        scratch_shapes=[pltpu.VMEM((d, 2 * d_expert), BF16), pltpu.VMEM((d_expert, d), BF16)],
    )
    return pl.pallas_call(
        functools.partial(_experts_kernel, d_expert=d_expert),
        grid_spec=grid_spec,
        out_shape=jax.ShapeDtypeStruct((p, d), F32),
        compiler_params=_params("arbitrary"),
        name="experts",
    )(block_e, n_valid, x_rows, row_w, w_gu, w_down)


def _shared_ln_kernel(xb_ref, x_ref, routed_ref, wgu_ref, wdn_ref, g_ref, b_ref, of_ref, ob_ref,
                      *, alpha, d_expert):
    h = jnp.dot(xb_ref[...], wgu_ref[...], preferred_element_type=F32)
    act = jax.nn.silu(h[:, :d_expert]) * h[:, d_expert:]
    shared = jnp.dot(act.astype(BF16), wdn_ref[...], preferred_element_type=F32)
    _layer_norm_store(alpha * x_ref[...] + (routed_ref[...] + shared), g_ref, b_ref, of_ref, ob_ref)


def _shared_ln(xb, x, routed, w_gu, w_down, g, b, alpha):
    m, d = x.shape
    d_expert = w_down.shape[0]
    tm = _tile(m, 256)
    row = lambda i: (i, 0)
    const = lambda i: (0, 0)
    return pl.pallas_call(
        functools.partial(_shared_ln_kernel, alpha=alpha, d_expert=d_expert),
        grid=(m // tm,),
        in_specs=[pl.BlockSpec((tm, d), row), pl.BlockSpec((tm, d), row), pl.BlockSpec((tm, d), row),
                  pl.BlockSpec((d, 2 * d_expert), const), pl.BlockSpec((d_expert, d), const),
                  pl.BlockSpec((1, d), const), pl.BlockSpec((1, d), const)],
        out_specs=[pl.BlockSpec((tm, d), row), pl.BlockSpec((tm, d), row)],
        out_shape=[jax.ShapeDtypeStruct((m, d), F32), jax.ShapeDtypeStruct((m, d), BF16)],
        compiler_params=_params("parallel"),
        name="shared_ln",
    )(xb, x, routed, w_gu, w_down, g.reshape(1, d), b.reshape(1, d))


def _route(scores, router_bias):
    n, n_experts = scores.shape
    sel_scores = scores + router_bias.astype(F32)
    grp = sel_scores.reshape(n, N_GROUPS, n_experts // N_GROUPS)
    grp_score = lax.top_k(grp, 2)[0].sum(-1)
    _, top_g = lax.top_k(grp_score, TOPK_GROUPS)
    g_mask = jax.nn.one_hot(top_g, N_GROUPS, dtype=F32).sum(-2) > 0
    e_mask = jnp.repeat(g_mask, n_experts // N_GROUPS, axis=-1)
    _, top_e = lax.top_k(jnp.where(e_mask, sel_scores, -jnp.inf), TOP_K)
    w = jnp.take_along_axis(scores, top_e, axis=-1)
    w = w / w.sum(-1, keepdims=True) * ROUTED_SCALE
    return top_e, w


def _moe(x, xb, router_w, router_bias, w_gu, w_down, sh_gu, sh_down, g, b, alpha):
    n, d = x.shape
    n_experts = router_w.shape[1]
    tm = EXPERT_ROW_BLOCK
    scores = _router(x, router_w)
    top_e, w = _route(scores, router_bias)

    chosen = jnp.zeros((n, n_experts), jnp.int32).at[jnp.arange(n)[:, None], top_e].set(1)
    before = jnp.cumsum(chosen, axis=0) - chosen
    counts = before[-1] + chosen[-1]
    padded = (counts + tm - 1) // tm * tm
    pend = jnp.cumsum(padded)
    pstart = pend - padded
    dest = pstart[top_e] + jnp.take_along_axis(before, top_e, axis=1)
    n_blocks = (n * TOP_K + n_experts * (tm - 1)) // tm + 1
    p = n_blocks * tm
    flat_dest = dest.reshape(-1)
    row_tok = jnp.zeros((p,), jnp.int32).at[flat_dest].set(
        jnp.repeat(jnp.arange(n, dtype=jnp.int32), TOP_K), unique_indices=True)
    row_w = jnp.zeros((p,), F32).at[flat_dest].set(w.reshape(-1), unique_indices=True)
    n_valid = (pend[-1] // tm).astype(jnp.int32)
    blk_start = jnp.minimum(jnp.arange(n_blocks, dtype=jnp.int32), n_valid - 1) * tm
    block_e = jnp.minimum(jnp.searchsorted(pend, blk_start, side='right'), n_experts - 1).astype(jnp.int32)

    x_rows = xb[row_tok]
    y_rows = _experts(x_rows, row_w.reshape(p, 1), block_e, n_valid.reshape(1), w_gu, w_down)
    routed = y_rows[dest].sum(axis=1)
    return _shared_ln(xb, x, routed, sh_gu, sh_down, g, b, alpha)


def kernel(x, ln_mix_g, ln_mix_b, ln_ffn_g, ln_ffn_b, a_w_in, a_conv_w, a_conv_b, a_gate_w, a_gate_b, a_lambda, a_w_out, kv_w, b_w_q, b_w_o, router_w, router_bias, moe_w_gu, moe_w_down, sh_w_gu, sh_w_down):
    batch, seq, d = x.shape
    n = batch * seq
    n_a = a_w_in.shape[0]
    n_b = b_w_q.shape[0]
    depth = n_a + n_b
    alpha = (2.0 * depth) ** 0.25
    d_attn = b_w_q.shape[2]

    inv = ROPE_THETA ** (-jnp.arange(0, HEAD_DIM, 2, dtype=F32) / HEAD_DIM)
    ang = jnp.arange(seq, dtype=F32)[:, None] * inv[None, :]
    ang = jnp.concatenate([ang, ang], axis=-1)
    cos = jnp.cos(ang)
    sign = jnp.where(jnp.arange(HEAD_DIM) < HEAD_DIM // 2, -1.0, 1.0).astype(F32)
    sin_signed = jnp.sin(ang) * sign

    xf = x.reshape(n, d)
    xb = xf.astype(BF16)
    kv = k_mean = None
    for layer in range(depth):
        if layer < n_a:
            proj = _proj(xb, a_w_in[layer].astype(BF16), F32)
            y = _rglru(proj, a_conv_w[layer], a_conv_b[layer], a_gate_w[layer].astype(BF16), a_gate_b[layer],
                       jax.nn.softplus(-a_lambda[layer].astype(F32)), batch, seq)
            w_o = a_w_out[layer]
        else:
            j = layer - n_a
            if j == 0:
                kv, k_mean = _rope_proj(xb, kv_w.astype(BF16), cos, sin_signed, d_attn, seq)
                k_mean = k_mean.reshape(batch, seq // MOBA_BLOCK, 2 * d_attn)
            q, _ = _rope_proj(xb, b_w_q[j].astype(BF16), cos, sin_signed, d_attn, seq)
            y = _moba(q, kv, k_mean, batch, seq)
            w_o = b_w_o[j]
        xf, xb = _mm_res_ln(y, w_o.astype(BF16), xf, ln_mix_g[layer], ln_mix_b[layer], alpha)
        xf, xb = _moe(xf, xb, router_w[layer], router_bias[layer], moe_w_gu[layer], moe_w_down[layer],
                      sh_w_gu[layer].astype(BF16), sh_w_down[layer].astype(BF16),
                      ln_ffn_g[layer], ln_ffn_b[layer], alpha)
    return xf.reshape(batch, seq, d)
```

```python
import functools
import math

import jax
import jax.numpy as jnp
from jax import lax
from jax.experimental import pallas as pl
from jax.experimental.pallas import tpu as pltpu

HEAD_DIM = 128
MOBA_BLOCK = 256
MOBA_TOPK = 3
ROPE_THETA = 10000.0
CONV_WIDTH = 4
LRU_C = 8.0
TOP_K = 8
N_GROUPS = 8
TOPK_GROUPS = 4
ROUTED_SCALE = 2.5
LN_EPS = 1e-5
NEG = -1e30

VMEM_LIMIT_BYTES = 56 * 1024 * 1024
SUBLANES = 8
EXPERT_ROW_BLOCK = 256

F32 = jnp.float32
BF16 = jnp.bfloat16


def _params(*sem):
    return pltpu.CompilerParams(dimension_semantics=sem, vmem_limit_bytes=VMEM_LIMIT_BYTES)


def _tile(n, pref):
    t = min(n, pref)
    assert n % t == 0, (n, t)
    return t


def _proj_kernel(x_ref, w_ref, o_ref):
    o_ref[...] = jnp.dot(x_ref[...], w_ref[...], preferred_element_type=F32).astype(o_ref.dtype)


def _proj(x, w, out_dtype):
    m, k = x.shape
    n = w.shape[1]
    tm, tn = _tile(m, 1024), _tile(n, 1024)
    return pl.pallas_call(
        _proj_kernel,
        grid=(m // tm, n // tn),
        in_specs=[pl.BlockSpec((tm, k), lambda i, j: (i, 0)),
                  pl.BlockSpec((k, tn), lambda i, j: (0, j))],
        out_specs=pl.BlockSpec((tm, tn), lambda i, j: (i, j)),
        out_shape=jax.ShapeDtypeStruct((m, n), out_dtype),
        compiler_params=_params("parallel", "arbitrary"),
        name="proj",
    )(x, w)


def _rope_proj_kernel(x_ref, w_ref, cos_ref, sin_ref, o_ref, km_ref, *, n_rope_tiles, tm, tn):
    acc = jnp.dot(x_ref[...], w_ref[...], preferred_element_type=F32)
    j = pl.program_id(1)
    nb = tm // MOBA_BLOCK

    def block_means(val, sl):
        for r in range(nb):
            km_ref[r, :, sl] = jnp.mean(val[r * MOBA_BLOCK:(r + 1) * MOBA_BLOCK], axis=0, keepdims=True)

    @pl.when(j < n_rope_tiles)
    def _():
        cos = cos_ref[...]
        sin = sin_ref[...]
        for h in range(tn // HEAD_DIM):
            sl = slice(h * HEAD_DIM, (h + 1) * HEAD_DIM)
            seg = acc[:, sl]
            roped = seg * cos + pltpu.roll(seg, HEAD_DIM // 2, axis=1) * sin
            o_ref[:, sl] = roped.astype(o_ref.dtype)
            block_means(roped, sl)

    @pl.when(j >= n_rope_tiles)
    def _():
        o_ref[...] = acc.astype(o_ref.dtype)
        block_means(acc, slice(None))


def _rope_proj(x, w, cos, sin_signed, n_rope, seq):
    m, k = x.shape
    n = w.shape[1]
    tm = _tile(seq, 1024)
    tn = _tile(n_rope, 1024)
    assert n % tn == 0 and tm % MOBA_BLOCK == 0
    n_seq_tiles = seq // tm
    kern = functools.partial(_rope_proj_kernel, n_rope_tiles=n_rope // tn, tm=tm, tn=tn)
    return pl.pallas_call(
        kern,
        grid=(m // tm, n // tn),
        in_specs=[pl.BlockSpec((tm, k), lambda i, j: (i, 0)),
                  pl.BlockSpec((k, tn), lambda i, j: (0, j)),
                  pl.BlockSpec((tm, HEAD_DIM), lambda i, j: (i % n_seq_tiles, 0)),
                  pl.BlockSpec((tm, HEAD_DIM), lambda i, j: (i % n_seq_tiles, 0))],
        out_specs=[pl.BlockSpec((tm, tn), lambda i, j: (i, j)),
                   pl.BlockSpec((tm // MOBA_BLOCK, 1, tn), lambda i, j: (i, 0, j))],
        out_shape=[jax.ShapeDtypeStruct((m, n), BF16),
                   jax.ShapeDtypeStruct((m // MOBA_BLOCK, 1, n), F32)],
        compiler_params=_params("parallel", "arbitrary"),
        name="rope_proj",
    )(x, w, cos, sin_signed)


def _layer_norm_store(z, g_ref, b_ref, of_ref, ob_ref):
    mu = jnp.mean(z, axis=-1, keepdims=True)
    zc = z - mu
    var = jnp.mean(zc * zc, axis=-1, keepdims=True)
    out = zc * lax.rsqrt(var + LN_EPS) * g_ref[...] + b_ref[...]
    of_ref[...] = out
    ob_ref[...] = out.astype(BF16)


def _mm_res_ln_kernel(y_ref, w_ref, x_ref, g_ref, b_ref, of_ref, ob_ref, *, alpha):
    mix = jnp.dot(y_ref[...], w_ref[...], preferred_element_type=F32)
    _layer_norm_store(alpha * x_ref[...] + mix, g_ref, b_ref, of_ref, ob_ref)


def _mm_res_ln(y, w, x, g, b, alpha):
    m, k = y.shape
    d = w.shape[1]
    tm = _tile(m, 256)
    row = lambda i: (i, 0)
    const = lambda i: (0, 0)
    return pl.pallas_call(
        functools.partial(_mm_res_ln_kernel, alpha=alpha),
        grid=(m // tm,),
        in_specs=[pl.BlockSpec((tm, k), row), pl.BlockSpec((k, d), const), pl.BlockSpec((tm, d), row),
                  pl.BlockSpec((1, d), const), pl.BlockSpec((1, d), const)],
        out_specs=[pl.BlockSpec((tm, d), row), pl.BlockSpec((tm, d), row)],
        out_shape=[jax.ShapeDtypeStruct((m, d), F32), jax.ShapeDtypeStruct((m, d), BF16)],
        compiler_params=_params("parallel"),
        name="mm_res_ln",
    )(y, w, x, g.reshape(1, d), b.reshape(1, d))


def _rglru_kernel(gb_ref, u_ref, cw_ref, cb_ref, gw_ref, gbias_ref, sp_ref, y_ref,
                  uext_ref, h_ref, a_ref, b_ref, *, ts, n_blocks, cblk):
    t = pl.program_id(1)

    @pl.when(t == 0)
    def _():
        uext_ref[0:SUBLANES, :] = jnp.zeros((SUBLANES, uext_ref.shape[1]), F32)
        h_ref[...] = jnp.zeros_like(h_ref)

    uext_ref[SUBLANES:SUBLANES + ts, :] = u_ref[...]
    u = cb_ref[...] + sum(
        uext_ref[pl.ds(SUBLANES - (CONV_WIDTH - 1) + k, ts), :] * cw_ref[k:k + 1, :]
        for k in range(CONV_WIDTH))
    uext_ref[0:SUBLANES, :] = uext_ref[ts:ts + SUBLANES, :]

    for n in range(n_blocks):
        sl = slice(n * cblk, (n + 1) * cblk)
        un = u[:, sl]
        gates = jnp.dot(un.astype(BF16), gw_ref[n], preferred_element_type=F32) + gbias_ref[n]
        gates = jax.nn.sigmoid(gates)
        r = gates[:, :cblk]
        i_gate = gates[:, cblk:]
        log_a = -LRU_C * r * sp_ref[:, sl]
        a = jnp.exp(log_a)
        mult = jnp.sqrt(1.0 - jnp.exp(2.0 * log_a))
        a_ref[:, sl] = a
        b_ref[:, sl] = mult * i_gate * un

    width = a_ref.shape[1]
    row = lax.broadcasted_iota(jnp.int32, (SUBLANES, width), 0)

    def slab(s, h):
        off = pl.multiple_of(s * SUBLANES, SUBLANES)
        a = a_ref[pl.ds(off, SUBLANES), :]
        b = b_ref[pl.ds(off, SUBLANES), :]
        for sh in (1, 2, 4):
            keep = row >= sh
            a_prev = jnp.where(keep, pltpu.roll(a, sh, axis=0), 1.0)
            b_prev = jnp.where(keep, pltpu.roll(b, sh, axis=0), 0.0)
            b = a * b_prev + b
            a = a * a_prev
        hs = a * h + b
        b_ref[pl.ds(off, SUBLANES), :] = hs
        return hs[SUBLANES - 1:SUBLANES, :]

    h_ref[...] = lax.fori_loop(0, ts // SUBLANES, slab, h_ref[...])
    y_ref[...] = (jax.nn.gelu(gb_ref[...], approximate=True) * b_ref[...]).astype(y_ref.dtype)


def _rglru(proj, conv_w, conv_b, gate_w, gate_b, softplus_neg_lam, batch, seq):
    m, two_d = proj.shape
    d = two_d // 2
    n_blocks, cblk = gate_w.shape[0], gate_w.shape[1]
    ts = _tile(seq, 256)
    nt = seq // ts
    kern = functools.partial(_rglru_kernel, ts=ts, n_blocks=n_blocks, cblk=cblk)
    const2 = lambda b, t: (0, 0)
    const3 = lambda b, t: (0, 0, 0)
    return pl.pallas_call(
        kern,
        grid=(batch, nt),
        in_specs=[pl.BlockSpec((ts, d), lambda b, t: (b * nt + t, 0)),
                  pl.BlockSpec((ts, d), lambda b, t: (b * nt + t, 1)),
                  pl.BlockSpec((CONV_WIDTH, d), const2),
                  pl.BlockSpec((1, d), const2),
                  pl.BlockSpec((n_blocks, cblk, 2 * cblk), const3),
                  pl.BlockSpec((n_blocks, 1, 2 * cblk), const3),
                  pl.BlockSpec((1, d), const2)],
        out_specs=pl.BlockSpec((ts, d), lambda b, t: (b * nt + t, 0)),
        out_shape=jax.ShapeDtypeStruct((m, d), BF16),
        scratch_shapes=[pltpu.VMEM((ts + SUBLANES, d), F32), pltpu.VMEM((1, d), F32),
                        pltpu.VMEM((ts, d), F32), pltpu.VMEM((ts, d), F32)],
        compiler_params=_params("arbitrary", "arbitrary"),
        name="rglru",
    )(proj, proj, conv_w, conv_b.reshape(1, d), gate_w, gate_b.reshape(n_blocks, 1, 2 * cblk),
      softplus_neg_lam.reshape(1, d))


def _moba_kernel(q_ref, k_ref, vt_ref, km_ref, o_ref, acc_ref, *, heads, n_kv_blocks):
    i = pl.program_id(2)
    L = MOBA_BLOCK
    nt_dims = (((1,), (1,)), ((), ()))
    blk = lax.broadcasted_iota(jnp.int32, (n_kv_blocks, L), 0)
    key = lax.broadcasted_iota(jnp.int32, (L, L), 0)
    qry = lax.broadcasted_iota(jnp.int32, (L, L), 1)
    sls = [slice(h * HEAD_DIM, (h + 1) * HEAD_DIM) for h in range(heads)]
    qs = [q_ref[:, sl] for sl in sls]

    bits = []
    for h in range(heads):
        gate = lax.dot_general(km_ref[0, :, sls[h]], qs[h].astype(F32), nt_dims,
                               precision=lax.Precision.HIGHEST, preferred_element_type=F32)
        gate = jnp.where(blk < i, gate, -jnp.inf)
        chosen = jnp.zeros((n_kv_blocks, L), F32)
        for _ in range(MOBA_TOPK):
            best = jnp.max(gate, axis=0, keepdims=True)
            first = jnp.min(jnp.where(gate == best, blk, n_kv_blocks), axis=0, keepdims=True)
            pick = (blk == first) & (best > -jnp.inf)
            chosen = jnp.where(pick, 1.0, chosen)
            gate = jnp.where(pick, -jnp.inf, gate)
        weight = jnp.left_shift(1, blk).astype(F32)
        bits.append(jnp.sum(chosen * weight, axis=0, keepdims=True).astype(jnp.int32))
        acc_ref[h] = jnp.zeros((HEAD_DIM, L), F32)

    def steps(koff, jblk, masks, carry):
        ss = [lax.dot_general(k_ref[pl.ds(koff, L), sls[h]], qs[h], nt_dims, preferred_element_type=F32)
              for h in range(heads)]
        ps, out = [], []
        for h in range(heads):
            m, l = carry[h]
            s = jnp.where(masks[h], ss[h], NEG)
            m_new = jnp.maximum(m, jnp.max(s, axis=0, keepdims=True))
            alpha = jnp.exp2(m - m_new)
            p = jnp.exp2(s - m_new)
            l = alpha * l + jnp.sum(p, axis=0, keepdims=True)
            acc_ref[h] = alpha * acc_ref[h]
            ps.append(p.astype(BF16))
            out.append((m_new, l))
        for h in range(heads):
            acc_ref[h] += jnp.dot(vt_ref[0, jblk, sls[h], :], ps[h], preferred_element_type=F32)
        return tuple(out)

    def past(j, carry):
        off = pl.multiple_of(j * L, L)
        masks = [(jnp.right_shift(bits[h], j) & 1) > 0 for h in range(heads)]
        return steps(off, j, masks, carry)

    init = tuple((jnp.full((1, L), NEG, F32), jnp.zeros((1, L), F32)) for _ in range(heads))
    carry = lax.fori_loop(0, i, past, init)
    off = pl.multiple_of(i * L, L)
    carry = steps(off, i, [key <= qry] * heads, carry)
    for h in range(heads):
        o_ref[:, sls[h]] = (acc_ref[h] / carry[h][1]).T.astype(o_ref.dtype)


def _moba(q, kv, vt, k_mean, batch, seq):
    m, d_attn = q.shape
    heads = min(8, d_attn // HEAD_DIM)
    gw = heads * HEAD_DIM
    n_groups = d_attn // gw
    nq = seq // MOBA_BLOCK
    kern = functools.partial(_moba_kernel, heads=heads, n_kv_blocks=nq)
    return pl.pallas_call(
        kern,
        grid=(batch, n_groups, nq),
        in_specs=[pl.BlockSpec((MOBA_BLOCK, gw), lambda b, g, i: (b * nq + i, g)),
                  pl.BlockSpec((seq, gw), lambda b, g, i: (b, g)),
                  pl.BlockSpec((1, nq, gw, MOBA_BLOCK), lambda b, g, i: (b, 0, g, 0)),
                  pl.BlockSpec((1, nq, gw), lambda b, g, i: (b, 0, g))],
        out_specs=pl.BlockSpec((MOBA_BLOCK, gw), lambda b, g, i: (b * nq + i, g)),
        out_shape=jax.ShapeDtypeStruct((m, d_attn), BF16),
        scratch_shapes=[pltpu.VMEM((heads, HEAD_DIM, MOBA_BLOCK), F32)],
        compiler_params=_params("parallel", "parallel", "arbitrary"),
        name="moba",
    )(q, kv, vt, k_mean)


def _router_kernel(x_ref, w_ref, bias_ref, tope_ref, w8_ref, rank8_ref, counts_ref, carry_ref, *, tm, n_experts):
    @pl.when(pl.program_id(0) == 0)
    def _():
        carry_ref[...] = jnp.zeros_like(carry_ref)

    logits = jnp.dot(x_ref[...], w_ref[...], precision=lax.Precision.HIGHEST, preferred_element_type=F32)
    scores = jax.nn.sigmoid(logits)
    sel = scores + bias_ref[...]
    lane = lax.broadcasted_iota(jnp.int32, (tm, n_experts), 1)
    group = lane // (n_experts // N_GROUPS)
    rmax = lambda v: jnp.max(v, axis=1, keepdims=True)
    first_of = lambda v, m: jnp.min(jnp.where(v == m, lane, n_experts), axis=1, keepdims=True)

    gscore = []
    for g in range(N_GROUPS):
        mg = jnp.where(group == g, sel, -jnp.inf)
        m1 = rmax(mg)
        m2 = rmax(jnp.where(lane == first_of(mg, m1), -jnp.inf, mg))
        gscore.append(m1 + m2)
    cand = jnp.full((tm, n_experts), -jnp.inf, F32)
    for g in range(N_GROUPS):
        beaten = jnp.zeros((tm, 1), jnp.int32)
        for o in range(N_GROUPS):
            if o != g:
                ahead = (gscore[o] > gscore[g]) | ((gscore[o] == gscore[g]) & (o < g))
                beaten = beaten + ahead.astype(jnp.int32)
        cand = jnp.where((group == g) & (beaten < TOPK_GROUPS), sel, cand)

    slot = lax.broadcasted_iota(jnp.int32, (tm, TOP_K), 1)
    chosen = jnp.zeros((tm, n_experts), F32)
    tope = jnp.zeros((tm, TOP_K), jnp.int32)
    s8 = jnp.zeros((tm, TOP_K), F32)
    picks = []
    for k in range(TOP_K):
        idx = first_of(cand, rmax(cand))
        hit = lane == idx
        picks.append(hit)
        chosen = jnp.where(hit, 1.0, chosen)
        cand = jnp.where(hit, -jnp.inf, cand)
        tope = jnp.where(slot == k, idx, tope)
        s8 = jnp.where(slot == k, jnp.sum(jnp.where(hit, scores, 0.0), axis=1, keepdims=True), s8)
    tope_ref[...] = tope
    w8_ref[...] = s8 / jnp.sum(s8, axis=1, keepdims=True) * ROUTED_SCALE

    r = lax.broadcasted_iota(jnp.int32, (tm, tm), 0)
    c = lax.broadcasted_iota(jnp.int32, (tm, tm), 1)
    before = jnp.dot((c < r).astype(BF16), chosen.astype(BF16), preferred_element_type=F32) + carry_ref[...]
    rank8 = jnp.zeros((tm, TOP_K), F32)
    for k in range(TOP_K):
        rank8 = jnp.where(slot == k, jnp.sum(jnp.where(picks[k], before, 0.0), axis=1, keepdims=True), rank8)
    rank8_ref[...] = rank8.astype(jnp.int32)
    carry_ref[...] += jnp.sum(chosen, axis=0, keepdims=True)
    counts_ref[...] = carry_ref[...].astype(jnp.int32)


def _router(x, w, bias):
    m, d = x.shape
    e = w.shape[1]
    tm = _tile(m, 256)
    row = lambda i: (i, 0)
    const = lambda i: (0, 0)
    return pl.pallas_call(
        functools.partial(_router_kernel, tm=tm, n_experts=e),
        grid=(m // tm,),
        in_specs=[pl.BlockSpec((tm, d), row), pl.BlockSpec((d, e), const), pl.BlockSpec((1, e), const)],
        out_specs=[pl.BlockSpec((tm, TOP_K), row), pl.BlockSpec((tm, TOP_K), row), pl.BlockSpec((tm, TOP_K), row),
                   pl.BlockSpec((1, e), const)],
        out_shape=[jax.ShapeDtypeStruct((m, TOP_K), jnp.int32), jax.ShapeDtypeStruct((m, TOP_K), F32),
                   jax.ShapeDtypeStruct((m, TOP_K), jnp.int32), jax.ShapeDtypeStruct((1, e), jnp.int32)],
        scratch_shapes=[pltpu.VMEM((1, e), F32)],
        compiler_params=_params("arbitrary"),
        name="router",
    )(x, w, bias.astype(F32).reshape(1, e))


def _experts_kernel(be_ref, nv_ref, x_ref, wgu_ref, wdn_ref, o_ref, wgu_bf, wdn_bf, *, d_expert):
    i = pl.program_id(0)
    prev = be_ref[jnp.maximum(i - 1, 0)]

    @pl.when((i == 0) | (be_ref[i] != prev))
    def _():
        wgu_bf[...] = wgu_ref[0, 0].astype(BF16)
        wdn_bf[...] = wdn_ref[0, 0].astype(BF16)

    @pl.when(i < nv_ref[0])
    def _():
        h = jnp.dot(x_ref[...], wgu_bf[...], preferred_element_type=F32)
        act = jax.nn.silu(h[:, :d_expert]) * h[:, d_expert:]
        o_ref[...] = jnp.dot(act.astype(BF16), wdn_bf[...], preferred_element_type=F32).astype(o_ref.dtype)


def _experts(x_rows, block_e, n_valid, w_gu, w_down, layer):
    p, d = x_rows.shape
    d_expert = w_down.shape[2]
    tm = EXPERT_ROW_BLOCK
    live = lambda i, be, nv: (jnp.minimum(i, nv[0] - 1), 0)
    weight = lambda i, be, nv: (layer, be[i], 0, 0)
    grid_spec = pltpu.PrefetchScalarGridSpec(
        num_scalar_prefetch=2,
        grid=(p // tm,),
        in_specs=[pl.BlockSpec((tm, d), live),
                  pl.BlockSpec((1, 1, d, 2 * d_expert), weight),
                  pl.BlockSpec((1, 1, d_expert, d), weight)],
        out_specs=pl.BlockSpec((tm, d), live),
        scratch_shapes=[pltpu.VMEM((d, 2 * d_expert), BF16), pltpu.VMEM((d_expert, d), BF16)],
    )
    return pl.pallas_call(
        functools.partial(_experts_kernel, d_expert=d_expert),
        grid_spec=grid_spec,
        out_shape=jax.ShapeDtypeStruct((p, d), BF16),
        compiler_params=_params("arbitrary"),
        name="experts",
    )(block_e, n_valid, x_rows, w_gu, w_down)


def _shared_ln_kernel(xb_ref, x_ref, yg_ref, w8_ref, wgu_ref, wdn_ref, g_ref, b_ref, of_ref, ob_ref,
                      *, alpha, d_expert, d):
    h = jnp.dot(xb_ref[...], wgu_ref[...], preferred_element_type=F32)
    act = jax.nn.silu(h[:, :d_expert]) * h[:, d_expert:]
    ffn = jnp.dot(act.astype(BF16), wdn_ref[...], preferred_element_type=F32)
    w8 = w8_ref[...]
    for k in range(TOP_K):
        ffn = ffn + w8[:, k:k + 1] * yg_ref[:, k * d:(k + 1) * d].astype(F32)
    _layer_norm_store(alpha * x_ref[...] + ffn, g_ref, b_ref, of_ref, ob_ref)


def _shared_ln(xb, x, yg, w8, w_gu, w_down, g, b, alpha):
    m, d = x.shape
    d_expert = w_down.shape[0]
    tm = _tile(m, 256)
    row = lambda i: (i, 0)
    const = lambda i: (0, 0)
    return pl.pallas_call(
        functools.partial(_shared_ln_kernel, alpha=alpha, d_expert=d_expert, d=d),
        grid=(m // tm,),
        in_specs=[pl.BlockSpec((tm, d), row), pl.BlockSpec((tm, d), row), pl.BlockSpec((tm, TOP_K * d), row),
                  pl.BlockSpec((tm, TOP_K), row),
                  pl.BlockSpec((d, 2 * d_expert), const), pl.BlockSpec((d_expert, d), const),
                  pl.BlockSpec((1, d), const), pl.BlockSpec((1, d), const)],
        out_specs=[pl.BlockSpec((tm, d), row), pl.BlockSpec((tm, d), row)],
        out_shape=[jax.ShapeDtypeStruct((m, d), F32), jax.ShapeDtypeStruct((m, d), BF16)],
        compiler_params=_params("parallel"),
        name="shared_ln",
    )(xb, x, yg, w8, w_gu, w_down, g.reshape(1, d), b.reshape(1, d))


def _moe(x, xb, router_w, router_bias, w_gu, w_down, layer, sh_gu, sh_down, g, b, alpha):
    n, d = x.shape
    n_experts = router_w.shape[1]
    tm = EXPERT_ROW_BLOCK
    top_e, w8, rank8, counts = _router(x, router_w, router_bias)

    counts = counts[0]
    padded = (counts + tm - 1) // tm * tm
    pend = jnp.cumsum(padded)
    pstart = pend - padded
    onehot = top_e[..., None] == jnp.arange(n_experts, dtype=jnp.int32)
    dest = (rank8 + jnp.sum(jnp.where(onehot, pstart, 0), axis=-1)).reshape(-1)
    n_blocks = (n * TOP_K + n_experts * (tm - 1)) // tm + 1
    p = n_blocks * tm
    row_tok = jnp.zeros((p,), jnp.int32).at[dest].set(
        jnp.repeat(jnp.arange(n, dtype=jnp.int32), TOP_K), unique_indices=True)
    n_valid = (pend[-1] // tm).astype(jnp.int32)
    blk_start = jnp.minimum(jnp.arange(n_blocks, dtype=jnp.int32), n_valid - 1) * tm
    block_e = jnp.minimum(jnp.sum((pend[None, :] <= blk_start[:, None]).astype(jnp.int32), axis=1), n_experts - 1)

    x_rows = xb[row_tok]
    y_rows = _experts(x_rows, block_e, n_valid.reshape(1), w_gu, w_down, layer)
    yg = y_rows[dest].reshape(n, TOP_K * d)
    return _shared_ln(xb, x, yg, w8, sh_gu, sh_down, g, b, alpha)


def kernel(x, ln_mix_g, ln_mix_b, ln_ffn_g, ln_ffn_b, a_w_in, a_conv_w, a_conv_b, a_gate_w, a_gate_b, a_lambda, a_w_out, kv_w, b_w_q, b_w_o, router_w, router_bias, moe_w_gu, moe_w_down, sh_w_gu, sh_w_down):
    batch, seq, d = x.shape
    n = batch * seq
    n_a = a_w_in.shape[0]
    n_b = b_w_q.shape[0]
    depth = n_a + n_b
    alpha = (2.0 * depth) ** 0.25
    d_attn = b_w_q.shape[2]

    inv = ROPE_THETA ** (-jnp.arange(0, HEAD_DIM, 2, dtype=F32) / HEAD_DIM)
    ang = jnp.arange(seq, dtype=F32)[:, None] * inv[None, :]
    ang = jnp.concatenate([ang, ang], axis=-1)
    cos = jnp.cos(ang)
    sign = jnp.where(jnp.arange(HEAD_DIM) < HEAD_DIM // 2, -1.0, 1.0).astype(F32)
    sin_signed = jnp.sin(ang) * sign
    q_scale = HEAD_DIM ** -0.5 * math.log2(math.e)

    xf = x.reshape(n, d)
    xb = xf.astype(BF16)
    kv = k_mean = v_t = None
    for layer in range(depth):
        if layer < n_a:
            proj = _proj(xb, a_w_in[layer].astype(BF16), F32)
            y = _rglru(proj, a_conv_w[layer], a_conv_b[layer], a_gate_w[layer].astype(BF16), a_gate_b[layer],
                       jax.nn.softplus(-a_lambda[layer].astype(F32)), batch, seq)
            w_o = a_w_out[layer]
        else:
            j = layer - n_a
            if j == 0:
                kv, k_mean = _rope_proj(xb, kv_w.astype(BF16), cos, sin_signed, d_attn, seq)
                k_mean = k_mean.reshape(batch, seq // MOBA_BLOCK, 2 * d_attn)
                v_t = jnp.swapaxes(kv[:, d_attn:].reshape(batch, seq // MOBA_BLOCK, MOBA_BLOCK, d_attn), 2, 3)
            q, _ = _rope_proj(xb, b_w_q[j].astype(BF16), cos * q_scale, sin_signed * q_scale, d_attn, seq)
            y = _moba(q, kv, v_t, k_mean, batch, seq)
            w_o = b_w_o[j]
        xf, xb = _mm_res_ln(y, w_o.astype(BF16), xf, ln_mix_g[layer], ln_mix_b[layer], alpha)
        xf, xb = _moe(xf, xb, router_w[layer], router_bias[layer], moe_w_gu, moe_w_down, layer,
                      sh_w_gu[layer].astype(BF16), sh_w_down[layer].astype(BF16),
                      ln_ffn_g[layer], ln_ffn_b[layer], alpha)
    return xf.reshape(batch, seq, d)
```

```python
import functools
import math

import jax
import jax.numpy as jnp
from jax import lax
from jax.experimental import pallas as pl
from jax.experimental.pallas import tpu as pltpu

HEAD_DIM = 128
MOBA_BLOCK = 256
MOBA_TOPK = 3
ROPE_THETA = 10000.0
CONV_WIDTH = 4
LRU_C = 8.0
TOP_K = 8
N_GROUPS = 8
TOPK_GROUPS = 4
ROUTED_SCALE = 2.5
LN_EPS = 1e-5
NEG = -1e30

VMEM_LIMIT_BYTES = 56 * 1024 * 1024
SUBLANES = 8
EXPERT_ROW_BLOCK = 256

F32 = jnp.float32
BF16 = jnp.bfloat16


def _params(*sem):
    return pltpu.CompilerParams(dimension_semantics=sem, vmem_limit_bytes=VMEM_LIMIT_BYTES)


def _tile(n, pref):
    t = min(n, pref)
    assert n % t == 0, (n, t)
    return t


def _proj_kernel(x_ref, w_ref, o_ref):
    o_ref[...] = jnp.dot(x_ref[...], w_ref[...], preferred_element_type=F32).astype(o_ref.dtype)


def _proj(x, w, out_dtype):
    m, k = x.shape
    n = w.shape[1]
    tm, tn = _tile(m, 1024), _tile(n, 1024)
    return pl.pallas_call(
        _proj_kernel,
        grid=(m // tm, n // tn),
        in_specs=[pl.BlockSpec((tm, k), lambda i, j: (i, 0)),
                  pl.BlockSpec((k, tn), lambda i, j: (0, j))],
        out_specs=pl.BlockSpec((tm, tn), lambda i, j: (i, j)),
        out_shape=jax.ShapeDtypeStruct((m, n), out_dtype),
        compiler_params=_params("parallel", "arbitrary"),
        name="proj",
    )(x, w)


def _rope_proj_kernel(x_ref, w_ref, cos_ref, sin_ref, o_ref, km_ref, *, n_rope_tiles, tm, tn):
    acc = jnp.dot(x_ref[...], w_ref[...], preferred_element_type=F32)
    j = pl.program_id(1)
    nb = tm // MOBA_BLOCK

    def block_means(val, sl):
        for r in range(nb):
            km_ref[r, :, sl] = jnp.mean(val[r * MOBA_BLOCK:(r + 1) * MOBA_BLOCK], axis=0, keepdims=True)

    @pl.when(j < n_rope_tiles)
    def _():
        cos = cos_ref[...]
        sin = sin_ref[...]
        for h in range(tn // HEAD_DIM):
            sl = slice(h * HEAD_DIM, (h + 1) * HEAD_DIM)
            seg = acc[:, sl]
            roped = seg * cos + pltpu.roll(seg, HEAD_DIM // 2, axis=1) * sin
            o_ref[:, sl] = roped.astype(o_ref.dtype)
            block_means(roped, sl)

    @pl.when(j >= n_rope_tiles)
    def _():
        o_ref[...] = acc.astype(o_ref.dtype)
        block_means(acc, slice(None))


def _rope_proj(x, w, cos, sin_signed, n_rope, seq):
    m, k = x.shape
    n = w.shape[1]
    tm = _tile(seq, 1024)
    tn = _tile(n_rope, 1024)
    assert n % tn == 0 and tm % MOBA_BLOCK == 0
    n_seq_tiles = seq // tm
    kern = functools.partial(_rope_proj_kernel, n_rope_tiles=n_rope // tn, tm=tm, tn=tn)
    return pl.pallas_call(
        kern,
        grid=(m // tm, n // tn),
        in_specs=[pl.BlockSpec((tm, k), lambda i, j: (i, 0)),
                  pl.BlockSpec((k, tn), lambda i, j: (0, j)),
                  pl.BlockSpec((tm, HEAD_DIM), lambda i, j: (i % n_seq_tiles, 0)),
                  pl.BlockSpec((tm, HEAD_DIM), lambda i, j: (i % n_seq_tiles, 0))],
        out_specs=[pl.BlockSpec((tm, tn), lambda i, j: (i, j)),
                   pl.BlockSpec((tm // MOBA_BLOCK, 1, tn), lambda i, j: (i, 0, j))],
        out_shape=[jax.ShapeDtypeStruct((m, n), BF16),
                   jax.ShapeDtypeStruct((m // MOBA_BLOCK, 1, n), F32)],
        compiler_params=_params("parallel", "arbitrary"),
        name="rope_proj",
    )(x, w, cos, sin_signed)


def _layer_norm_store(z, g_ref, b_ref, of_ref, ob_ref):
    mu = jnp.mean(z, axis=-1, keepdims=True)
    zc = z - mu
    var = jnp.mean(zc * zc, axis=-1, keepdims=True)
    out = zc * lax.rsqrt(var + LN_EPS) * g_ref[...] + b_ref[...]
    of_ref[...] = out
    ob_ref[...] = out.astype(BF16)


def _mm_res_ln_kernel(y_ref, w_ref, x_ref, g_ref, b_ref, of_ref, ob_ref, *, alpha):
    mix = jnp.dot(y_ref[...], w_ref[...], preferred_element_type=F32)
    _layer_norm_store(alpha * x_ref[...] + mix, g_ref, b_ref, of_ref, ob_ref)


def _mm_res_ln(y, w, x, g, b, alpha):
    m, k = y.shape
    d = w.shape[1]
    tm = _tile(m, 256)
    row = lambda i: (i, 0)
    const = lambda i: (0, 0)
    return pl.pallas_call(
        functools.partial(_mm_res_ln_kernel, alpha=alpha),
        grid=(m // tm,),
        in_specs=[pl.BlockSpec((tm, k), row), pl.BlockSpec((k, d), const), pl.BlockSpec((tm, d), row),
                  pl.BlockSpec((1, d), const), pl.BlockSpec((1, d), const)],
        out_specs=[pl.BlockSpec((tm, d), row), pl.BlockSpec((tm, d), row)],
        out_shape=[jax.ShapeDtypeStruct((m, d), F32), jax.ShapeDtypeStruct((m, d), BF16)],
        compiler_params=_params("parallel"),
        name="mm_res_ln",
    )(y, w, x, g.reshape(1, d), b.reshape(1, d))


def _rglru_kernel(gb_ref, u_ref, cw_ref, cb_ref, gw_ref, gbias_ref, sp_ref, y_ref,
                  uext_ref, h_ref, a_ref, b_ref, *, ts, n_blocks, cblk):
    t = pl.program_id(1)

    @pl.when(t == 0)
    def _():
        uext_ref[0:SUBLANES, :] = jnp.zeros((SUBLANES, uext_ref.shape[1]), F32)
        h_ref[...] = jnp.zeros_like(h_ref)

    uext_ref[SUBLANES:SUBLANES + ts, :] = u_ref[...]
    u = cb_ref[...] + sum(
        uext_ref[pl.ds(SUBLANES - (CONV_WIDTH - 1) + k, ts), :] * cw_ref[k:k + 1, :]
        for k in range(CONV_WIDTH))
    uext_ref[0:SUBLANES, :] = uext_ref[ts:ts + SUBLANES, :]

    for n in range(n_blocks):
        sl = slice(n * cblk, (n + 1) * cblk)
        un = u[:, sl]
        gates = jnp.dot(un.astype(BF16), gw_ref[n], preferred_element_type=F32) + gbias_ref[n]
        gates = jax.nn.sigmoid(gates)
        r = gates[:, :cblk]
        i_gate = gates[:, cblk:]
        log_a = -LRU_C * r * sp_ref[:, sl]
        a = jnp.exp(log_a)
        mult = jnp.sqrt(1.0 - jnp.exp(2.0 * log_a))
        a_ref[:, sl] = a
        b_ref[:, sl] = mult * i_gate * un

    width = a_ref.shape[1]
    row = lax.broadcasted_iota(jnp.int32, (SUBLANES, width), 0)

    def slab(s, h):
        off = pl.multiple_of(s * SUBLANES, SUBLANES)
        a = a_ref[pl.ds(off, SUBLANES), :]
        b = b_ref[pl.ds(off, SUBLANES), :]
        for sh in (1, 2, 4):
            keep = row >= sh
            a_prev = jnp.where(keep, pltpu.roll(a, sh, axis=0), 1.0)
            b_prev = jnp.where(keep, pltpu.roll(b, sh, axis=0), 0.0)
            b = a * b_prev + b
            a = a * a_prev
        hs = a * h + b
        b_ref[pl.ds(off, SUBLANES), :] = hs
        return hs[SUBLANES - 1:SUBLANES, :]

    h_ref[...] = lax.fori_loop(0, ts // SUBLANES, slab, h_ref[...])
    y_ref[...] = (jax.nn.gelu(gb_ref[...], approximate=True) * b_ref[...]).astype(y_ref.dtype)


def _rglru(proj, conv_w, conv_b, gate_w, gate_b, softplus_neg_lam, batch, seq):
    m, two_d = proj.shape
    d = two_d // 2
    n_blocks, cblk = gate_w.shape[0], gate_w.shape[1]
    ts = _tile(seq, 256)
    nt = seq // ts
    kern = functools.partial(_rglru_kernel, ts=ts, n_blocks=n_blocks, cblk=cblk)
    const2 = lambda b, t: (0, 0)
    const3 = lambda b, t: (0, 0, 0)
    return pl.pallas_call(
        kern,
        grid=(batch, nt),
        in_specs=[pl.BlockSpec((ts, d), lambda b, t: (b * nt + t, 0)),
                  pl.BlockSpec((ts, d), lambda b, t: (b * nt + t, 1)),
                  pl.BlockSpec((CONV_WIDTH, d), const2),
                  pl.BlockSpec((1, d), const2),
                  pl.BlockSpec((n_blocks, cblk, 2 * cblk), const3),
                  pl.BlockSpec((n_blocks, 1, 2 * cblk), const3),
                  pl.BlockSpec((1, d), const2)],
        out_specs=pl.BlockSpec((ts, d), lambda b, t: (b * nt + t, 0)),
        out_shape=jax.ShapeDtypeStruct((m, d), BF16),
        scratch_shapes=[pltpu.VMEM((ts + SUBLANES, d), F32), pltpu.VMEM((1, d), F32),
                        pltpu.VMEM((ts, d), F32), pltpu.VMEM((ts, d), F32)],
        compiler_params=_params("arbitrary", "arbitrary"),
        name="rglru",
    )(proj, proj, conv_w, conv_b.reshape(1, d), gate_w, gate_b.reshape(n_blocks, 1, 2 * cblk),
      softplus_neg_lam.reshape(1, d))


def _moba_kernel(q_ref, k_ref, vt_ref, km_ref, o_ref, acc_ref, *, heads, n_kv_blocks):
    i = pl.program_id(2)
    L = MOBA_BLOCK
    nt_dims = (((1,), (1,)), ((), ()))
    blk = lax.broadcasted_iota(jnp.int32, (n_kv_blocks, L), 0)
    key = lax.broadcasted_iota(jnp.int32, (L, L), 0)
    qry = lax.broadcasted_iota(jnp.int32, (L, L), 1)
    sls = [slice(h * HEAD_DIM, (h + 1) * HEAD_DIM) for h in range(heads)]
    qs = [q_ref[:, sl] for sl in sls]

    bits = []
    for h in range(heads):
        gate = lax.dot_general(km_ref[0, :, sls[h]], qs[h].astype(F32), nt_dims,
                               precision=lax.Precision.HIGHEST, preferred_element_type=F32)
        gate = jnp.where(blk < i, gate, -jnp.inf)
        chosen = jnp.zeros((n_kv_blocks, L), F32)
        for _ in range(MOBA_TOPK):
            best = jnp.max(gate, axis=0, keepdims=True)
            first = jnp.min(jnp.where(gate == best, blk, n_kv_blocks), axis=0, keepdims=True)
            pick = (blk == first) & (best > -jnp.inf)
            chosen = jnp.where(pick, 1.0, chosen)
            gate = jnp.where(pick, -jnp.inf, gate)
        weight = jnp.left_shift(1, blk).astype(F32)
        bits.append(jnp.sum(chosen * weight, axis=0, keepdims=True).astype(jnp.int32))
        acc_ref[h] = jnp.zeros((HEAD_DIM, L), F32)

    def steps(koff, jblk, masks, carry):
        ss = [lax.dot_general(k_ref[pl.ds(koff, L), sls[h]], qs[h], nt_dims, preferred_element_type=F32)
              for h in range(heads)]
        ps, out = [], []
        for h in range(heads):
            m, l = carry[h]
            s = jnp.where(masks[h], ss[h], NEG)
            m_new = jnp.maximum(m, jnp.max(s, axis=0, keepdims=True))
            alpha = jnp.exp2(m - m_new)
            p = jnp.exp2(s - m_new)
            l = alpha * l + jnp.sum(p, axis=0, keepdims=True)
            acc_ref[h] = alpha * acc_ref[h]
            ps.append(p.astype(BF16))
            out.append((m_new, l))
        for h in range(heads):
            acc_ref[h] += jnp.dot(vt_ref[0, jblk, sls[h], :], ps[h], preferred_element_type=F32)
        return tuple(out)

    def past(j, carry):
        off = pl.multiple_of(j * L, L)
        masks = [(jnp.right_shift(bits[h], j) & 1) > 0 for h in range(heads)]
        return steps(off, j, masks, carry)

    init = tuple((jnp.full((1, L), NEG, F32), jnp.zeros((1, L), F32)) for _ in range(heads))
    carry = lax.fori_loop(0, i, past, init)
    off = pl.multiple_of(i * L, L)
    carry = steps(off, i, [key <= qry] * heads, carry)
    for h in range(heads):
        o_ref[:, sls[h]] = (acc_ref[h] / carry[h][1]).T.astype(o_ref.dtype)


def _moba(q, kv, vt, k_mean, batch, seq):
    m, d_attn = q.shape
    heads = min(8, d_attn // HEAD_DIM)
    gw = heads * HEAD_DIM
    n_groups = d_attn // gw
    nq = seq // MOBA_BLOCK
    kern = functools.partial(_moba_kernel, heads=heads, n_kv_blocks=nq)
    return pl.pallas_call(
        kern,
        grid=(batch, n_groups, nq),
        in_specs=[pl.BlockSpec((MOBA_BLOCK, gw), lambda b, g, i: (b * nq + i, g)),
                  pl.BlockSpec((seq, gw), lambda b, g, i: (b, g)),
                  pl.BlockSpec((1, nq, gw, MOBA_BLOCK), lambda b, g, i: (b, 0, g, 0)),
                  pl.BlockSpec((1, nq, gw), lambda b, g, i: (b, 0, g))],
        out_specs=pl.BlockSpec((MOBA_BLOCK, gw), lambda b, g, i: (b * nq + i, g)),
        out_shape=jax.ShapeDtypeStruct((m, d_attn), BF16),
        scratch_shapes=[pltpu.VMEM((heads, HEAD_DIM, MOBA_BLOCK), F32)],
        compiler_params=_params("parallel", "parallel", "arbitrary"),
        name="moba",
    )(q, kv, vt, k_mean)


def _router_kernel(x_ref, w_ref, bias_ref, tope_ref, w8_ref, rank8_ref, counts_ref, carry_ref, *, tm, n_experts):
    @pl.when(pl.program_id(0) == 0)
    def _():
        carry_ref[...] = jnp.zeros_like(carry_ref)

    logits = jnp.dot(x_ref[...], w_ref[...], precision=lax.Precision.HIGHEST, preferred_element_type=F32)
    scores = jax.nn.sigmoid(logits)
    sel = scores + bias_ref[...]
    lane = lax.broadcasted_iota(jnp.int32, (tm, n_experts), 1)
    group = lane // (n_experts // N_GROUPS)
    rmax = lambda v: jnp.max(v, axis=1, keepdims=True)
    first_of = lambda v, m: jnp.min(jnp.where(v == m, lane, n_experts), axis=1, keepdims=True)

    gscore = []
    for g in range(N_GROUPS):
        mg = jnp.where(group == g, sel, -jnp.inf)
        m1 = rmax(mg)
        m2 = rmax(jnp.where(lane == first_of(mg, m1), -jnp.inf, mg))
        gscore.append(m1 + m2)
    cand = jnp.full((tm, n_experts), -jnp.inf, F32)
    for g in range(N_GROUPS):
        beaten = jnp.zeros((tm, 1), jnp.int32)
        for o in range(N_GROUPS):
            if o != g:
                ahead = (gscore[o] > gscore[g]) | ((gscore[o] == gscore[g]) & (o < g))
                beaten = beaten + ahead.astype(jnp.int32)
        cand = jnp.where((group == g) & (beaten < TOPK_GROUPS), sel, cand)

    slot = lax.broadcasted_iota(jnp.int32, (tm, TOP_K), 1)
    chosen = jnp.zeros((tm, n_experts), F32)
    tope = jnp.zeros((tm, TOP_K), jnp.int32)
    s8 = jnp.zeros((tm, TOP_K), F32)
    picks = []
    for k in range(TOP_K):
        idx = first_of(cand, rmax(cand))
        hit = lane == idx
        picks.append(hit)
        chosen = jnp.where(hit, 1.0, chosen)
        cand = jnp.where(hit, -jnp.inf, cand)
        tope = jnp.where(slot == k, idx, tope)
        s8 = jnp.where(slot == k, jnp.sum(jnp.where(hit, scores, 0.0), axis=1, keepdims=True), s8)
    tope_ref[...] = tope
    w8_ref[...] = s8 / jnp.sum(s8, axis=1, keepdims=True) * ROUTED_SCALE

    r = lax.broadcasted_iota(jnp.int32, (tm, tm), 0)
    c = lax.broadcasted_iota(jnp.int32, (tm, tm), 1)
    before = jnp.dot((c < r).astype(BF16), chosen.astype(BF16), preferred_element_type=F32) + carry_ref[...]
    rank8 = jnp.zeros((tm, TOP_K), F32)
    for k in range(TOP_K):
        rank8 = jnp.where(slot == k, jnp.sum(jnp.where(picks[k], before, 0.0), axis=1, keepdims=True), rank8)
    rank8_ref[...] = rank8.astype(jnp.int32)
    carry_ref[...] += jnp.sum(chosen, axis=0, keepdims=True)
    counts_ref[...] = carry_ref[...].astype(jnp.int32)


def _router(x, w, bias):
    m, d = x.shape
    e = w.shape[1]
    tm = _tile(m, 256)
    row = lambda i: (i, 0)
    const = lambda i: (0, 0)
    return pl.pallas_call(
        functools.partial(_router_kernel, tm=tm, n_experts=e),
        grid=(m // tm,),
        in_specs=[pl.BlockSpec((tm, d), row), pl.BlockSpec((d, e), const), pl.BlockSpec((1, e), const)],
        out_specs=[pl.BlockSpec((tm, TOP_K), row), pl.BlockSpec((tm, TOP_K), row), pl.BlockSpec((tm, TOP_K), row),
                   pl.BlockSpec((1, e), const)],
        out_shape=[jax.ShapeDtypeStruct((m, TOP_K), jnp.int32), jax.ShapeDtypeStruct((m, TOP_K), F32),
                   jax.ShapeDtypeStruct((m, TOP_K), jnp.int32), jax.ShapeDtypeStruct((1, e), jnp.int32)],
        scratch_shapes=[pltpu.VMEM((1, e), F32)],
        compiler_params=_params("arbitrary"),
        name="router",
    )(x, w, bias.astype(F32).reshape(1, e))


def _dispatch_kernel(dest_ref, x_ref, init_ref, o_ref, sem, *, tm):
    del init_ref

    def send(r, carry):
        for k in range(TOP_K):
            d = dest_ref[0, 0, r * TOP_K + k]
            pltpu.make_async_copy(x_ref.at[pl.ds(r, 1), :], o_ref.at[pl.ds(d, 1), :], sem).start()
        return carry

    lax.fori_loop(0, tm, send, 0)
    for _ in range(TOP_K):
        pltpu.make_async_copy(x_ref, o_ref.at[pl.ds(0, tm), :], sem).wait()


def _dispatch(x, dest, p):
    n, d = x.shape
    tm = _tile(n, 256)
    return pl.pallas_call(
        functools.partial(_dispatch_kernel, tm=tm),
        grid=(n // tm,),
        in_specs=[pl.BlockSpec((1, 1, tm * TOP_K), lambda i: (i, 0, 0), memory_space=pltpu.SMEM),
                  pl.BlockSpec((tm, d), lambda i: (i, 0)),
                  pl.BlockSpec(memory_space=pl.ANY)],
        out_specs=pl.BlockSpec(memory_space=pl.ANY),
        out_shape=jax.ShapeDtypeStruct((p, d), x.dtype),
        scratch_shapes=[pltpu.SemaphoreType.DMA(())],
        input_output_aliases={2: 0},
        compiler_params=_params("arbitrary"),
        name="dispatch",
    )(dest.reshape(n // tm, 1, tm * TOP_K), x, jnp.zeros((p, d), x.dtype))


def _experts_kernel(be_ref, nv_ref, x_ref, wgu_ref, wdn_ref, o_ref, wgu_bf, wdn_bf, *, d_expert):
    i = pl.program_id(0)
    prev = be_ref[jnp.maximum(i - 1, 0)]

    @pl.when((i == 0) | (be_ref[i] != prev))
    def _():
        wgu_bf[...] = wgu_ref[0, 0].astype(BF16)
        wdn_bf[...] = wdn_ref[0, 0].astype(BF16)

    @pl.when(i < nv_ref[0])
    def _():
        h = jnp.dot(x_ref[...].astype(BF16), wgu_bf[...], preferred_element_type=F32)
        act = jax.nn.silu(h[:, :d_expert]) * h[:, d_expert:]
        o_ref[...] = jnp.dot(act.astype(BF16), wdn_bf[...], preferred_element_type=F32).astype(o_ref.dtype)

    @pl.when(i >= nv_ref[0])
    def _():
        o_ref[...] = jnp.zeros_like(o_ref)


def _experts(x_rows, block_e, n_valid, w_gu, w_down, layer):
    p, d = x_rows.shape
    d_expert = w_down.shape[2]
    tm = EXPERT_ROW_BLOCK
    live = lambda i, be, nv: (jnp.maximum(jnp.minimum(i, nv[0] - 1), 0), 0)
    weight = lambda i, be, nv: (layer, be[i], 0, 0)
    grid_spec = pltpu.PrefetchScalarGridSpec(
        num_scalar_prefetch=2,
        grid=(p // tm,),
        in_specs=[pl.BlockSpec((tm, d), live),
                  pl.BlockSpec((1, 1, d, 2 * d_expert), weight),
                  pl.BlockSpec((1, 1, d_expert, d), weight)],
        out_specs=pl.BlockSpec((tm, d), lambda i, be, nv: (i, 0)),
        scratch_shapes=[pltpu.VMEM((d, 2 * d_expert), BF16), pltpu.VMEM((d_expert, d), BF16)],
    )
    return pl.pallas_call(
        functools.partial(_experts_kernel, d_expert=d_expert),
        grid_spec=grid_spec,
        out_shape=jax.ShapeDtypeStruct((p, d), BF16),
        compiler_params=_params("arbitrary"),
        name="experts",
    )(block_e, n_valid, x_rows, w_gu, w_down)


def _shared_ln_kernel(xb_ref, x_ref, yg_ref, w8_ref, wgu_ref, wdn_ref, g_ref, b_ref, of_ref, ob_ref,
                      *, alpha, d_expert):
    h = jnp.dot(xb_ref[...], wgu_ref[...], preferred_element_type=F32)
    act = jax.nn.silu(h[:, :d_expert]) * h[:, d_expert:]
    ffn = jnp.dot(act.astype(BF16), wdn_ref[...], preferred_element_type=F32)
    w8 = w8_ref[...]
    for k in range(TOP_K):
        ffn = ffn + w8[:, k:k + 1] * yg_ref[k].astype(F32)
    _layer_norm_store(alpha * x_ref[...] + ffn, g_ref, b_ref, of_ref, ob_ref)


def _shared_ln(xb, x, yg, w8, w_gu, w_down, g, b, alpha):
    m, d = x.shape
    d_expert = w_down.shape[0]
    tm = _tile(m, 256)
    row = lambda i: (i, 0)
    const = lambda i: (0, 0)
    return pl.pallas_call(
        functools.partial(_shared_ln_kernel, alpha=alpha, d_expert=d_expert),
        grid=(m // tm,),
        in_specs=[pl.BlockSpec((tm, d), row), pl.BlockSpec((tm, d), row),
                  pl.BlockSpec((TOP_K, tm, d), lambda i: (0, i, 0)),
                  pl.BlockSpec((tm, TOP_K), row),
                  pl.BlockSpec((d, 2 * d_expert), const), pl.BlockSpec((d_expert, d), const),
                  pl.BlockSpec((1, d), const), pl.BlockSpec((1, d), const)],
        out_specs=[pl.BlockSpec((tm, d), row), pl.BlockSpec((tm, d), row)],
        out_shape=[jax.ShapeDtypeStruct((m, d), F32), jax.ShapeDtypeStruct((m, d), BF16)],
        compiler_params=_params("parallel"),
        name="shared_ln",
    )(xb, x, yg, w8, w_gu, w_down, g.reshape(1, d), b.reshape(1, d))


def _moe(x, xb, router_w, router_bias, w_gu, w_down, layer, sh_gu, sh_down, g, b, alpha):
    n, d = x.shape
    n_experts = router_w.shape[1]
    tm = EXPERT_ROW_BLOCK
    top_e, w8, rank8, counts = _router(x, router_w, router_bias)

    counts = counts[0]
    padded = (counts + tm - 1) // tm * tm
    pend = jnp.cumsum(padded)
    pstart = pend - padded
    onehot = top_e[..., None] == jnp.arange(n_experts, dtype=jnp.int32)
    dest = rank8 + jnp.sum(jnp.where(onehot, pstart, 0), axis=-1)
    n_blocks = (n * TOP_K + n_experts * (tm - 1)) // tm + 1
    n_valid = (pend[-1] // tm).astype(jnp.int32)
    blk_start = jnp.minimum(jnp.arange(n_blocks, dtype=jnp.int32), n_valid - 1) * tm
    block_e = jnp.minimum(jnp.sum((pend[None, :] <= blk_start[:, None]).astype(jnp.int32), axis=1), n_experts - 1)

    x_rows = _dispatch(x, dest, n_blocks * tm)
    y_rows = _experts(x_rows, block_e, n_valid.reshape(1), w_gu, w_down, layer)
    yg = y_rows[dest.T.reshape(-1)].reshape(TOP_K, n, d)
    return _shared_ln(xb, x, yg, w8, sh_gu, sh_down, g, b, alpha)


def kernel(x, ln_mix_g, ln_mix_b, ln_ffn_g, ln_ffn_b, a_w_in, a_conv_w, a_conv_b, a_gate_w, a_gate_b, a_lambda, a_w_out, kv_w, b_w_q, b_w_o, router_w, router_bias, moe_w_gu, moe_w_down, sh_w_gu, sh_w_down):
    batch, seq, d = x.shape
    n = batch * seq
    n_a = a_w_in.shape[0]
    n_b = b_w_q.shape[0]
    depth = n_a + n_b
    alpha = (2.0 * depth) ** 0.25
    d_attn = b_w_q.shape[2]

    inv = ROPE_THETA ** (-jnp.arange(0, HEAD_DIM, 2, dtype=F32) / HEAD_DIM)
    ang = jnp.arange(seq, dtype=F32)[:, None] * inv[None, :]
    ang = jnp.concatenate([ang, ang], axis=-1)
    cos = jnp.cos(ang)
    sign = jnp.where(jnp.arange(HEAD_DIM) < HEAD_DIM // 2, -1.0, 1.0).astype(F32)
    sin_signed = jnp.sin(ang) * sign
    q_scale = HEAD_DIM ** -0.5 * math.log2(math.e)

    xf = x.reshape(n, d)
    xb = xf.astype(BF16)
    kv = k_mean = v_t = None
    for layer in range(depth):
        if layer < n_a:
            proj = _proj(xb, a_w_in[layer].astype(BF16), F32)
            y = _rglru(proj, a_conv_w[layer], a_conv_b[layer], a_gate_w[layer].astype(BF16), a_gate_b[layer],
                       jax.nn.softplus(-a_lambda[layer].astype(F32)), batch, seq)
            w_o = a_w_out[layer]
        else:
            j = layer - n_a
            if j == 0:
                kv, k_mean = _rope_proj(xb, kv_w.astype(BF16), cos, sin_signed, d_attn, seq)
                k_mean = k_mean.reshape(batch, seq // MOBA_BLOCK, 2 * d_attn)
                v_t = jnp.swapaxes(kv[:, d_attn:].reshape(batch, seq // MOBA_BLOCK, MOBA_BLOCK, d_attn), 2, 3)
            q, _ = _rope_proj(xb, b_w_q[j].astype(BF16), cos * q_scale, sin_signed * q_scale, d_attn, seq)
            y = _moba(q, kv, v_t, k_mean, batch, seq)
            w_o = b_w_o[j]
        xf, xb = _mm_res_ln(y, w_o.astype(BF16), xf, ln_mix_g[layer], ln_mix_b[layer], alpha)
        xf, xb = _moe(xf, xb, router_w[layer], router_bias[layer], moe_w_gu, moe_w_down, layer,
                      sh_w_gu[layer].astype(BF16), sh_w_down[layer].astype(BF16),
                      ln_ffn_g[layer], ln_ffn_b[layer], alpha)
    return xf.reshape(batch, seq, d)
```

```python
import functools
import math

import jax
import jax.numpy as jnp
from jax import lax
from jax.experimental import pallas as pl
from jax.experimental.pallas import tpu as pltpu

HEAD_DIM = 128
MOBA_BLOCK = 256
MOBA_TOPK = 3
ROPE_THETA = 10000.0
CONV_WIDTH = 4
LRU_C = 8.0
TOP_K = 8
N_GROUPS = 8
TOPK_GROUPS = 4
ROUTED_SCALE = 2.5
LN_EPS = 1e-5
NEG = -1e30

VMEM_LIMIT_BYTES = 56 * 1024 * 1024
SUBLANES = 8
EXPERT_ROW_BLOCK = 256

F32 = jnp.float32
BF16 = jnp.bfloat16


def _params(*sem):
    return pltpu.CompilerParams(dimension_semantics=sem, vmem_limit_bytes=VMEM_LIMIT_BYTES)


def _tile(n, pref):
    t = min(n, pref)
    assert n % t == 0, (n, t)
    return t


def _proj_kernel(x_ref, w_ref, o_ref):
    o_ref[...] = jnp.dot(x_ref[...], w_ref[...], preferred_element_type=F32).astype(o_ref.dtype)


def _proj(x, w, out_dtype):
    m, k = x.shape
    n = w.shape[1]
    tm, tn = _tile(m, 1024), _tile(n, 1024)
    return pl.pallas_call(
        _proj_kernel,
        grid=(m // tm, n // tn),
        in_specs=[pl.BlockSpec((tm, k), lambda i, j: (i, 0)),
                  pl.BlockSpec((k, tn), lambda i, j: (0, j))],
        out_specs=pl.BlockSpec((tm, tn), lambda i, j: (i, j)),
        out_shape=jax.ShapeDtypeStruct((m, n), out_dtype),
        compiler_params=_params("parallel", "arbitrary"),
        name="proj",
    )(x, w)


def _rope_proj_kernel(x_ref, w_ref, cos_ref, sin_ref, o_ref, km_ref, *, n_rope_tiles, tm, tn):
    acc = jnp.dot(x_ref[...], w_ref[...], preferred_element_type=F32)
    j = pl.program_id(1)
    nb = tm // MOBA_BLOCK

    def block_means(val, sl):
        for r in range(nb):
            km_ref[r, :, sl] = jnp.mean(val[r * MOBA_BLOCK:(r + 1) * MOBA_BLOCK], axis=0, keepdims=True)

    @pl.when(j < n_rope_tiles)
    def _():
        cos = cos_ref[...]
        sin = sin_ref[...]
        for h in range(tn // HEAD_DIM):
            sl = slice(h * HEAD_DIM, (h + 1) * HEAD_DIM)
            seg = acc[:, sl]
            roped = seg * cos + pltpu.roll(seg, HEAD_DIM // 2, axis=1) * sin
            o_ref[:, sl] = roped.astype(o_ref.dtype)
            block_means(roped, sl)

    @pl.when(j >= n_rope_tiles)
    def _():
        o_ref[...] = acc.astype(o_ref.dtype)
        block_means(acc, slice(None))


def _rope_proj(x, w, cos, sin_signed, n_rope, seq):
    m, k = x.shape
    n = w.shape[1]
    tm = _tile(seq, 1024)
    tn = _tile(n_rope, 1024)
    assert n % tn == 0 and tm % MOBA_BLOCK == 0
    n_seq_tiles = seq // tm
    kern = functools.partial(_rope_proj_kernel, n_rope_tiles=n_rope // tn, tm=tm, tn=tn)
    return pl.pallas_call(
        kern,
        grid=(m // tm, n // tn),
        in_specs=[pl.BlockSpec((tm, k), lambda i, j: (i, 0)),
                  pl.BlockSpec((k, tn), lambda i, j: (0, j)),
                  pl.BlockSpec((tm, HEAD_DIM), lambda i, j: (i % n_seq_tiles, 0)),
                  pl.BlockSpec((tm, HEAD_DIM), lambda i, j: (i % n_seq_tiles, 0))],
        out_specs=[pl.BlockSpec((tm, tn), lambda i, j: (i, j)),
                   pl.BlockSpec((tm // MOBA_BLOCK, 1, tn), lambda i, j: (i, 0, j))],
        out_shape=[jax.ShapeDtypeStruct((m, n), BF16),
                   jax.ShapeDtypeStruct((m // MOBA_BLOCK, 1, n), F32)],
        compiler_params=_params("parallel", "arbitrary"),
        name="rope_proj",
    )(x, w, cos, sin_signed)


def _layer_norm_store(z, g_ref, b_ref, of_ref, ob_ref):
    mu = jnp.mean(z, axis=-1, keepdims=True)
    zc = z - mu
    var = jnp.mean(zc * zc, axis=-1, keepdims=True)
    out = zc * lax.rsqrt(var + LN_EPS) * g_ref[...] + b_ref[...]
    of_ref[...] = out
    ob_ref[...] = out.astype(BF16)


def _mm_res_ln_kernel(y_ref, w_ref, x_ref, g_ref, b_ref, of_ref, ob_ref, *, alpha):
    mix = jnp.dot(y_ref[...], w_ref[...], preferred_element_type=F32)
    _layer_norm_store(alpha * x_ref[...] + mix, g_ref, b_ref, of_ref, ob_ref)


def _mm_res_ln(y, w, x, g, b, alpha):
    m, k = y.shape
    d = w.shape[1]
    tm = _tile(m, 256)
    row = lambda i: (i, 0)
    const = lambda i: (0, 0)
    return pl.pallas_call(
        functools.partial(_mm_res_ln_kernel, alpha=alpha),
        grid=(m // tm,),
        in_specs=[pl.BlockSpec((tm, k), row), pl.BlockSpec((k, d), const), pl.BlockSpec((tm, d), row),
                  pl.BlockSpec((1, d), const), pl.BlockSpec((1, d), const)],
        out_specs=[pl.BlockSpec((tm, d), row), pl.BlockSpec((tm, d), row)],
        out_shape=[jax.ShapeDtypeStruct((m, d), F32), jax.ShapeDtypeStruct((m, d), BF16)],
        compiler_params=_params("parallel"),
        name="mm_res_ln",
    )(y, w, x, g.reshape(1, d), b.reshape(1, d))


def _rglru_kernel(gb_ref, u_ref, cw_ref, cb_ref, gw_ref, gbias_ref, sp_ref, y_ref,
                  uext_ref, h_ref, a_ref, b_ref, *, ts, n_blocks, cblk):
    t = pl.program_id(1)

    @pl.when(t == 0)
    def _():
        uext_ref[0:SUBLANES, :] = jnp.zeros((SUBLANES, uext_ref.shape[1]), F32)
        h_ref[...] = jnp.zeros_like(h_ref)

    uext_ref[SUBLANES:SUBLANES + ts, :] = u_ref[...]
    u = cb_ref[...] + sum(
        uext_ref[pl.ds(SUBLANES - (CONV_WIDTH - 1) + k, ts), :] * cw_ref[k:k + 1, :]
        for k in range(CONV_WIDTH))
    uext_ref[0:SUBLANES, :] = uext_ref[ts:ts + SUBLANES, :]

    for n in range(n_blocks):
        sl = slice(n * cblk, (n + 1) * cblk)
        un = u[:, sl]
        gates = jnp.dot(un.astype(BF16), gw_ref[n], preferred_element_type=F32) + gbias_ref[n]
        gates = jax.nn.sigmoid(gates)
        r = gates[:, :cblk]
        i_gate = gates[:, cblk:]
        log_a = -LRU_C * r * sp_ref[:, sl]
        a = jnp.exp(log_a)
        mult = jnp.sqrt(1.0 - jnp.exp(2.0 * log_a))
        a_ref[:, sl] = a
        b_ref[:, sl] = mult * i_gate * un

    width = a_ref.shape[1]
    row = lax.broadcasted_iota(jnp.int32, (SUBLANES, width), 0)

    def slab(s, h):
        off = pl.multiple_of(s * SUBLANES, SUBLANES)
        a = a_ref[pl.ds(off, SUBLANES), :]
        b = b_ref[pl.ds(off, SUBLANES), :]
        for sh in (1, 2, 4):
            keep = row >= sh
            a_prev = jnp.where(keep, pltpu.roll(a, sh, axis=0), 1.0)
            b_prev = jnp.where(keep, pltpu.roll(b, sh, axis=0), 0.0)
            b = a * b_prev + b
            a = a * a_prev
        hs = a * h + b
        b_ref[pl.ds(off, SUBLANES), :] = hs
        return hs[SUBLANES - 1:SUBLANES, :]

    h_ref[...] = lax.fori_loop(0, ts // SUBLANES, slab, h_ref[...])
    y_ref[...] = (jax.nn.gelu(gb_ref[...], approximate=True) * b_ref[...]).astype(y_ref.dtype)


def _rglru(proj, conv_w, conv_b, gate_w, gate_b, softplus_neg_lam, batch, seq):
    m, two_d = proj.shape
    d = two_d // 2
    n_blocks, cblk = gate_w.shape[0], gate_w.shape[1]
    ts = _tile(seq, 256)
    nt = seq // ts
    kern = functools.partial(_rglru_kernel, ts=ts, n_blocks=n_blocks, cblk=cblk)
    const2 = lambda b, t: (0, 0)
    const3 = lambda b, t: (0, 0, 0)
    return pl.pallas_call(
        kern,
        grid=(batch, nt),
        in_specs=[pl.BlockSpec((ts, d), lambda b, t: (b * nt + t, 0)),
                  pl.BlockSpec((ts, d), lambda b, t: (b * nt + t, 1)),
                  pl.BlockSpec((CONV_WIDTH, d), const2),
                  pl.BlockSpec((1, d), const2),
                  pl.BlockSpec((n_blocks, cblk, 2 * cblk), const3),
                  pl.BlockSpec((n_blocks, 1, 2 * cblk), const3),
                  pl.BlockSpec((1, d), const2)],
        out_specs=pl.BlockSpec((ts, d), lambda b, t: (b * nt + t, 0)),
        out_shape=jax.ShapeDtypeStruct((m, d), BF16),
        scratch_shapes=[pltpu.VMEM((ts + SUBLANES, d), F32), pltpu.VMEM((1, d), F32),
                        pltpu.VMEM((ts, d), F32), pltpu.VMEM((ts, d), F32)],
        compiler_params=_params("arbitrary", "arbitrary"),
        name="rglru",
    )(proj, proj, conv_w, conv_b.reshape(1, d), gate_w, gate_b.reshape(n_blocks, 1, 2 * cblk),
      softplus_neg_lam.reshape(1, d))


def _moba_kernel(q_ref, k_ref, vt_ref, km_ref, o_ref, acc_ref, *, heads, n_kv_blocks):
    i = pl.program_id(2)
    L = MOBA_BLOCK
    nt_dims = (((1,), (1,)), ((), ()))
    blk = lax.broadcasted_iota(jnp.int32, (n_kv_blocks, L), 0)
    key = lax.broadcasted_iota(jnp.int32, (L, L), 0)
    qry = lax.broadcasted_iota(jnp.int32, (L, L), 1)
    sls = [slice(h * HEAD_DIM, (h + 1) * HEAD_DIM) for h in range(heads)]
    qs = [q_ref[:, sl] for sl in sls]

    bits = []
    for h in range(heads):
        gate = lax.dot_general(km_ref[0, :, sls[h]], qs[h].astype(F32), nt_dims,
                               precision=lax.Precision.HIGHEST, preferred_element_type=F32)
        gate = jnp.where(blk < i, gate, -jnp.inf)
        chosen = jnp.zeros((n_kv_blocks, L), F32)
        for _ in range(MOBA_TOPK):
            best = jnp.max(gate, axis=0, keepdims=True)
            first = jnp.min(jnp.where(gate == best, blk, n_kv_blocks), axis=0, keepdims=True)
            pick = (blk == first) & (best > -jnp.inf)
            chosen = jnp.where(pick, 1.0, chosen)
            gate = jnp.where(pick, -jnp.inf, gate)
        weight = jnp.left_shift(1, blk).astype(F32)
        bits.append(jnp.sum(chosen * weight, axis=0, keepdims=True).astype(jnp.int32))
        acc_ref[h] = jnp.zeros((HEAD_DIM, L), F32)

    def steps(koff, jblk, masks, carry):
        ss = [lax.dot_general(k_ref[pl.ds(koff, L), sls[h]], qs[h], nt_dims, preferred_element_type=F32)
              for h in range(heads)]
        ps, out = [], []
        for h in range(heads):
            m, l = carry[h]
            s = jnp.where(masks[h], ss[h], NEG)
            m_new = jnp.maximum(m, jnp.max(s, axis=0, keepdims=True))
            alpha = jnp.exp2(m - m_new)
            p = jnp.exp2(s - m_new)
            l = alpha * l + jnp.sum(p, axis=0, keepdims=True)
            acc_ref[h] = alpha * acc_ref[h]
            ps.append(p.astype(BF16))
            out.append((m_new, l))
        for h in range(heads):
            acc_ref[h] += jnp.dot(vt_ref[0, jblk, sls[h], :], ps[h], preferred_element_type=F32)
        return tuple(out)

    def past(j, carry):
        off = pl.multiple_of(j * L, L)
        masks = [(jnp.right_shift(bits[h], j) & 1) > 0 for h in range(heads)]
        return steps(off, j, masks, carry)

    init = tuple((jnp.full((1, L), NEG, F32), jnp.zeros((1, L), F32)) for _ in range(heads))
    carry = lax.fori_loop(0, i, past, init)
    off = pl.multiple_of(i * L, L)
    carry = steps(off, i, [key <= qry] * heads, carry)
    for h in range(heads):
        o_ref[:, sls[h]] = (acc_ref[h] / carry[h][1]).T.astype(o_ref.dtype)


def _moba(q, kv, vt, k_mean, batch, seq):
    m, d_attn = q.shape
    heads = min(8, d_attn // HEAD_DIM)
    gw = heads * HEAD_DIM
    n_groups = d_attn // gw
    nq = seq // MOBA_BLOCK
    kern = functools.partial(_moba_kernel, heads=heads, n_kv_blocks=nq)
    return pl.pallas_call(
        kern,
        grid=(batch, n_groups, nq),
        in_specs=[pl.BlockSpec((MOBA_BLOCK, gw), lambda b, g, i: (b * nq + i, g)),
                  pl.BlockSpec((seq, gw), lambda b, g, i: (b, g)),
                  pl.BlockSpec((1, nq, gw, MOBA_BLOCK), lambda b, g, i: (b, 0, g, 0)),
                  pl.BlockSpec((1, nq, gw), lambda b, g, i: (b, 0, g))],
        out_specs=pl.BlockSpec((MOBA_BLOCK, gw), lambda b, g, i: (b * nq + i, g)),
        out_shape=jax.ShapeDtypeStruct((m, d_attn), BF16),
        scratch_shapes=[pltpu.VMEM((heads, HEAD_DIM, MOBA_BLOCK), F32)],
        compiler_params=_params("parallel", "parallel", "arbitrary"),
        name="moba",
    )(q, kv, vt, k_mean)


def _router_kernel(x_ref, whi_ref, wlo_ref, bias_ref, tope_ref, w8_ref, rank8_ref, counts_ref, carry_ref,
                   *, tm, n_experts):
    @pl.when(pl.program_id(0) == 0)
    def _():
        carry_ref[...] = jnp.zeros_like(carry_ref)

    nt_dims = (((1,), (1,)), ((), ()))
    x = x_ref[...]
    xh = x.astype(BF16)
    xl = (x - xh.astype(F32)).astype(BF16)
    ntdot = lambda a, b: lax.dot_general(a, b, nt_dims, preferred_element_type=F32)
    logits = ntdot(whi_ref[...], xh) + (ntdot(wlo_ref[...], xh) + ntdot(whi_ref[...], xl))
    scores = jax.nn.sigmoid(logits)
    sel = scores + bias_ref[...]

    gsz = n_experts // N_GROUPS
    cmax = lambda v: jnp.max(v, axis=0, keepdims=True)
    csum = lambda v: jnp.sum(v, axis=0, keepdims=True)
    first_of = lambda v, m, idx, n: jnp.min(jnp.where(v == m, idx, float(n)), axis=0, keepdims=True)

    in_group = lax.broadcasted_iota(jnp.int32, (gsz, tm), 0).astype(F32)
    groups = [sel[g * gsz:(g + 1) * gsz, :] for g in range(N_GROUPS)]
    gscore = []
    for sg in groups:
        m1 = cmax(sg)
        m2 = cmax(jnp.where(in_group == first_of(sg, m1, in_group, gsz), -jnp.inf, sg))
        gscore.append(m1 + m2)
    gidx = lax.broadcasted_iota(jnp.int32, (N_GROUPS, tm), 0)
    gs = jnp.concatenate(gscore, axis=0)
    beaten = jnp.zeros((N_GROUPS, tm), jnp.int32)
    for o in range(N_GROUPS):
        ahead = (gscore[o] > gs) | ((gscore[o] == gs) & (o < gidx))
        beaten = beaten + ahead.astype(jnp.int32)
    keep = beaten < TOPK_GROUPS
    cand = jnp.concatenate([jnp.where(keep[g:g + 1, :], groups[g], -jnp.inf) for g in range(N_GROUPS)], axis=0)

    eidx = lax.broadcasted_iota(jnp.int32, (n_experts, tm), 0).astype(F32)
    chosen = jnp.zeros((n_experts, tm), F32)
    picks, tope_rows, score_rows = [], [], []
    for _ in range(TOP_K):
        idx = first_of(cand, cmax(cand), eidx, n_experts)
        hit = eidx == idx
        picks.append(hit)
        tope_rows.append(idx)
        score_rows.append(csum(jnp.where(hit, scores, 0.0)))
        chosen = jnp.where(hit, 1.0, chosen)
        cand = jnp.where(hit, -jnp.inf, cand)
    tope_ref[...] = jnp.concatenate(tope_rows, axis=0).astype(jnp.int32)
    s8 = jnp.concatenate(score_rows, axis=0)
    w8_ref[...] = s8 / csum(s8) * ROUTED_SCALE

    r = lax.broadcasted_iota(jnp.int32, (tm, tm), 0)
    c = lax.broadcasted_iota(jnp.int32, (tm, tm), 1)
    before = jnp.dot(chosen.astype(BF16), (r < c).astype(BF16), preferred_element_type=F32) + carry_ref[...]
    rank_rows = [csum(jnp.where(picks[k], before, 0.0)) for k in range(TOP_K)]
    rank8_ref[...] = jnp.concatenate(rank_rows, axis=0).astype(jnp.int32)
    carry_ref[...] += jnp.sum(chosen, axis=1, keepdims=True)
    counts_ref[...] = carry_ref[...].astype(jnp.int32)


def _router(x, w, bias):
    m, d = x.shape
    e = w.shape[1]
    tm = _tile(m, 256)
    wt = w.astype(F32).T
    w_hi = wt.astype(BF16)
    w_lo = (wt - w_hi.astype(F32)).astype(BF16)
    const = lambda i: (0, 0)
    col = lambda i: (0, i)
    return pl.pallas_call(
        functools.partial(_router_kernel, tm=tm, n_experts=e),
        grid=(m // tm,),
        in_specs=[pl.BlockSpec((tm, d), lambda i: (i, 0)), pl.BlockSpec((e, d), const), pl.BlockSpec((e, d), const),
                  pl.BlockSpec((e, 1), const)],
        out_specs=[pl.BlockSpec((TOP_K, tm), col), pl.BlockSpec((TOP_K, tm), col), pl.BlockSpec((TOP_K, tm), col),
                   pl.BlockSpec((e, 1), const)],
        out_shape=[jax.ShapeDtypeStruct((TOP_K, m), jnp.int32), jax.ShapeDtypeStruct((TOP_K, m), F32),
                   jax.ShapeDtypeStruct((TOP_K, m), jnp.int32), jax.ShapeDtypeStruct((e, 1), jnp.int32)],
        scratch_shapes=[pltpu.VMEM((e, 1), F32)],
        compiler_params=_params("arbitrary"),
        name="router",
    )(x, w_hi, w_lo, bias.astype(F32).reshape(e, 1))


def _dispatch_kernel(dest_ref, x_ref, init_ref, o_ref, sem, *, tm):
    del init_ref

    def send(r, carry):
        for k in range(TOP_K):
            d = dest_ref[0, 0, r * TOP_K + k]
            pltpu.make_async_copy(x_ref.at[pl.ds(r, 1), :], o_ref.at[pl.ds(d, 1), :], sem).start()
        return carry

    lax.fori_loop(0, tm, send, 0)
    for _ in range(TOP_K):
        pltpu.make_async_copy(x_ref, o_ref.at[pl.ds(0, tm), :], sem).wait()


def _dispatch(x, dest, init):
    n, d = x.shape
    tm = _tile(n, 256)
    assert init.shape[1] == d and init.dtype == x.dtype
    return pl.pallas_call(
        functools.partial(_dispatch_kernel, tm=tm),
        grid=(n // tm,),
        in_specs=[pl.BlockSpec((1, 1, tm * TOP_K), lambda i: (i, 0, 0), memory_space=pltpu.SMEM),
                  pl.BlockSpec((tm, d), lambda i: (i, 0)),
                  pl.BlockSpec(memory_space=pl.ANY)],
        out_specs=pl.BlockSpec(memory_space=pl.ANY),
        out_shape=jax.ShapeDtypeStruct(init.shape, x.dtype),
        scratch_shapes=[pltpu.SemaphoreType.DMA(())],
        input_output_aliases={2: 0},
        compiler_params=_params("arbitrary"),
        name="dispatch",
    )(dest.reshape(n // tm, 1, tm * TOP_K), x, init)


def _experts_kernel(be_ref, nxt_ref, slot_ref, nv_ref, x_ref, wgu_hbm, wdn_hbm, o_ref,
                    wgu_f32, wdn_f32, wgu_bf, wdn_bf, sems, *, d_expert, layer):
    i = pl.program_id(0)
    e = be_ref[i]
    slot = slot_ref[i]
    live = i < nv_ref[0]

    def weight_copies(expert, s):
        return (pltpu.make_async_copy(wgu_hbm.at[layer, expert], wgu_f32.at[s], sems.at[0, s]),
                pltpu.make_async_copy(wdn_hbm.at[layer, expert], wdn_f32.at[s], sems.at[1, s]))

    @pl.when(live & (i == 0))
    def _():
        for cp in weight_copies(e, slot):
            cp.start()

    @pl.when(live & ((i == 0) | (e != be_ref[jnp.maximum(i - 1, 0)])))
    def _():
        for cp in weight_copies(e, slot):
            cp.wait()
        nxt = nxt_ref[i]

        @pl.when(nxt >= 0)
        def _():
            for cp in weight_copies(nxt, 1 - slot):
                cp.start()

        wgu_bf[...] = wgu_f32[slot].astype(BF16)
        wdn_bf[...] = wdn_f32[slot].astype(BF16)

    @pl.when(live)
    def _():
        h = jnp.dot(x_ref[...].astype(BF16), wgu_bf[...], preferred_element_type=F32)
        act = jax.nn.silu(h[:, :d_expert]) * h[:, d_expert:]
        o_ref[...] = jnp.dot(act.astype(BF16), wdn_bf[...], preferred_element_type=F32).astype(o_ref.dtype)

    @pl.when(jnp.logical_not(live))
    def _():
        o_ref[...] = jnp.zeros_like(o_ref)


def _experts(x_rows, block_e, next_e, slot, n_valid, w_gu, w_down, layer):
    p, d = x_rows.shape
    d_expert = w_down.shape[2]
    tm = EXPERT_ROW_BLOCK
    grid_spec = pltpu.PrefetchScalarGridSpec(
        num_scalar_prefetch=4,
        grid=(p // tm,),
        in_specs=[pl.BlockSpec((tm, d), lambda i, be, nx, sl, nv: (jnp.maximum(jnp.minimum(i, nv[0] - 1), 0), 0)),
                  pl.BlockSpec(memory_space=pl.ANY),
                  pl.BlockSpec(memory_space=pl.ANY)],
        out_specs=pl.BlockSpec((tm, d), lambda i, be, nx, sl, nv: (i, 0)),
        scratch_shapes=[pltpu.VMEM((2, d, 2 * d_expert), F32), pltpu.VMEM((2, d_expert, d), F32),
                        pltpu.VMEM((d, 2 * d_expert), BF16), pltpu.VMEM((d_expert, d), BF16),
                        pltpu.SemaphoreType.DMA((2, 2))],
    )
    return pl.pallas_call(
        functools.partial(_experts_kernel, d_expert=d_expert, layer=layer),
        grid_spec=grid_spec,
        out_shape=jax.ShapeDtypeStruct((p, d), BF16),
        compiler_params=_params("arbitrary"),
        name="experts",
    )(block_e, next_e, slot, n_valid, x_rows, w_gu, w_down)


def _shared_ln_kernel(xb_ref, x_ref, yg_ref, w8_ref, wgu_ref, wdn_ref, g_ref, b_ref, of_ref, ob_ref,
                      *, alpha, d_expert):
    h = jnp.dot(xb_ref[...], wgu_ref[...], preferred_element_type=F32)
    act = jax.nn.silu(h[:, :d_expert]) * h[:, d_expert:]
    ffn = jnp.dot(act.astype(BF16), wdn_ref[...], preferred_element_type=F32)
    w8 = w8_ref[...]
    for k in range(TOP_K):
        ffn = ffn + w8[:, k:k + 1] * yg_ref[k].astype(F32)
    _layer_norm_store(alpha * x_ref[...] + ffn, g_ref, b_ref, of_ref, ob_ref)


def _shared_ln(xb, x, yg, w8, w_gu, w_down, g, b, alpha):
    m, d = x.shape
    d_expert = w_down.shape[0]
    tm = _tile(m, 256)
    row = lambda i: (i, 0)
    const = lambda i: (0, 0)
    return pl.pallas_call(
        functools.partial(_shared_ln_kernel, alpha=alpha, d_expert=d_expert),
        grid=(m // tm,),
        in_specs=[pl.BlockSpec((tm, d), row), pl.BlockSpec((tm, d), row),
                  pl.BlockSpec((TOP_K, tm, d), lambda i: (0, i, 0)),
                  pl.BlockSpec((tm, TOP_K), row),
                  pl.BlockSpec((d, 2 * d_expert), const), pl.BlockSpec((d_expert, d), const),
                  pl.BlockSpec((1, d), const), pl.BlockSpec((1, d), const)],
        out_specs=[pl.BlockSpec((tm, d), row), pl.BlockSpec((tm, d), row)],
        out_shape=[jax.ShapeDtypeStruct((m, d), F32), jax.ShapeDtypeStruct((m, d), BF16)],
        compiler_params=_params("parallel"),
        name="shared_ln",
    )(xb, x, yg, w8, w_gu, w_down, g.reshape(1, d), b.reshape(1, d))


def _moe(x, xb, router_w, router_bias, w_gu, w_down, layer, sh_gu, sh_down, g, b, alpha, rows_buf):
    n, d = x.shape
    n_experts = router_w.shape[1]
    tm = EXPERT_ROW_BLOCK
    top_e, w8, rank8, counts = _router(x, router_w, router_bias)

    counts = counts[:, 0]
    experts = jnp.arange(n_experts, dtype=jnp.int32)
    padded = (counts + tm - 1) // tm * tm
    pend = jnp.cumsum(padded)
    pstart = pend - padded
    dest = rank8 + jnp.sum(jnp.where(top_e[..., None] == experts, pstart, 0), axis=-1)
    n_blocks = (n * TOP_K + n_experts * (tm - 1)) // tm + 1
    n_valid = (pend[-1] // tm).astype(jnp.int32)
    blk_start = jnp.minimum(jnp.arange(n_blocks, dtype=jnp.int32), n_valid - 1) * tm
    block_e = jnp.minimum(jnp.sum((pend[None, :] <= blk_start[:, None]).astype(jnp.int32), axis=1), n_experts - 1)
    nonempty = counts > 0
    later = (experts[None, :] > block_e[:, None]) & nonempty[None, :]
    next_e = jnp.min(jnp.where(later, experts[None, :], n_experts), axis=1)
    next_e = jnp.where(next_e == n_experts, -1, next_e).astype(jnp.int32)
    earlier = (experts[None, :] < block_e[:, None]) & nonempty[None, :]
    slot = (jnp.sum(earlier.astype(jnp.int32), axis=1) % 2).astype(jnp.int32)

    if rows_buf is None:
        rows_buf = jnp.zeros((n_blocks * tm, d), x.dtype)
    x_rows = _dispatch(x, dest.T, rows_buf)
    y_rows = _experts(x_rows, block_e, next_e, slot, n_valid.reshape(1), w_gu, w_down, layer)
    yg = y_rows[dest.reshape(-1)].reshape(TOP_K, n, d)
    xf, xb = _shared_ln(xb, x, yg, w8.T, sh_gu, sh_down, g, b, alpha)
    return xf, xb, x_rows


def kernel(x, ln_mix_g, ln_mix_b, ln_ffn_g, ln_ffn_b, a_w_in, a_conv_w, a_conv_b, a_gate_w, a_gate_b, a_lambda, a_w_out, kv_w, b_w_q, b_w_o, router_w, router_bias, moe_w_gu, moe_w_down, sh_w_gu, sh_w_down):
    batch, seq, d = x.shape
    n = batch * seq
    n_a = a_w_in.shape[0]
    n_b = b_w_q.shape[0]
    depth = n_a + n_b
    alpha = (2.0 * depth) ** 0.25
    d_attn = b_w_q.shape[2]

    inv = ROPE_THETA ** (-jnp.arange(0, HEAD_DIM, 2, dtype=F32) / HEAD_DIM)
    ang = jnp.arange(seq, dtype=F32)[:, None] * inv[None, :]
    ang = jnp.concatenate([ang, ang], axis=-1)
    cos = jnp.cos(ang)
    sign = jnp.where(jnp.arange(HEAD_DIM) < HEAD_DIM // 2, -1.0, 1.0).astype(F32)
    sin_signed = jnp.sin(ang) * sign
    q_scale = HEAD_DIM ** -0.5 * math.log2(math.e)

    xf = x.reshape(n, d)
    xb = xf.astype(BF16)
    kv = k_mean = v_t = rows_buf = None
    for layer in range(depth):
        if layer < n_a:
            proj = _proj(xb, a_w_in[layer].astype(BF16), F32)
            y = _rglru(proj, a_conv_w[layer], a_conv_b[layer], a_gate_w[layer].astype(BF16), a_gate_b[layer],
                       jax.nn.softplus(-a_lambda[layer].astype(F32)), batch, seq)
            w_o = a_w_out[layer]
        else:
            j = layer - n_a
            if j == 0:
                kv, k_mean = _rope_proj(xb, kv_w.astype(BF16), cos, sin_signed, d_attn, seq)
                k_mean = k_mean.reshape(batch, seq // MOBA_BLOCK, 2 * d_attn)
                v_t = jnp.swapaxes(kv[:, d_attn:].reshape(batch, seq // MOBA_BLOCK, MOBA_BLOCK, d_attn), 2, 3)
            q, _ = _rope_proj(xb, b_w_q[j].astype(BF16), cos * q_scale, sin_signed * q_scale, d_attn, seq)
            y = _moba(q, kv, v_t, k_mean, batch, seq)
            w_o = b_w_o[j]
        xf, xb = _mm_res_ln(y, w_o.astype(BF16), xf, ln_mix_g[layer], ln_mix_b[layer], alpha)
        xf, xb, rows_buf = _moe(xf, xb, router_w[layer], router_bias[layer], moe_w_gu, moe_w_down, layer,
                                sh_w_gu[layer].astype(BF16), sh_w_down[layer].astype(BF16),
                                ln_ffn_g[layer], ln_ffn_b[layer], alpha, rows_buf)
    return xf.reshape(batch, seq, d)
```

```python
import functools
import math

import jax
import jax.numpy as jnp
from jax import lax
from jax.experimental import pallas as pl
from jax.experimental.pallas import tpu as pltpu

HEAD_DIM = 128
MOBA_BLOCK = 256
MOBA_TOPK = 3
ROPE_THETA = 10000.0
CONV_WIDTH = 4
LRU_C = 8.0
TOP_K = 8
N_GROUPS = 8
TOPK_GROUPS = 4
ROUTED_SCALE = 2.5
LN_EPS = 1e-5
NEG = -1e30

VMEM_LIMIT_BYTES = 56 * 1024 * 1024
SUBLANES = 8
EXPERT_ROW_BLOCK = 256

F32 = jnp.float32
BF16 = jnp.bfloat16


def _params(*sem):
    return pltpu.CompilerParams(dimension_semantics=sem, vmem_limit_bytes=VMEM_LIMIT_BYTES)


def _tile(n, pref):
    t = min(n, pref)
    assert n % t == 0, (n, t)
    return t


def _proj_kernel(x_ref, w_ref, o_ref):
    o_ref[...] = jnp.dot(x_ref[...], w_ref[...], preferred_element_type=F32).astype(o_ref.dtype)


def _proj(x, w, out_dtype):
    m, k = x.shape
    n = w.shape[1]
    tm, tn = _tile(m, 1024), _tile(n, 1024)
    return pl.pallas_call(
        _proj_kernel,
        grid=(m // tm, n // tn),
        in_specs=[pl.BlockSpec((tm, k), lambda i, j: (i, 0)),
                  pl.BlockSpec((k, tn), lambda i, j: (0, j))],
        out_specs=pl.BlockSpec((tm, tn), lambda i, j: (i, j)),
        out_shape=jax.ShapeDtypeStruct((m, n), out_dtype),
        compiler_params=_params("parallel", "arbitrary"),
        name="proj",
    )(x, w)


def _rope_proj_kernel(x_ref, w_ref, cos_ref, sin_ref, o_ref, km_ref, *, n_rope_tiles, tm, tn):
    acc = jnp.dot(x_ref[...], w_ref[...], preferred_element_type=F32)
    j = pl.program_id(1)
    nb = tm // MOBA_BLOCK

    def block_means(val, sl):
        for r in range(nb):
            km_ref[r, :, sl] = jnp.mean(val[r * MOBA_BLOCK:(r + 1) * MOBA_BLOCK], axis=0, keepdims=True)

    @pl.when(j < n_rope_tiles)
    def _():
        cos = cos_ref[...]
        sin = sin_ref[...]
        for h in range(tn // HEAD_DIM):
            sl = slice(h * HEAD_DIM, (h + 1) * HEAD_DIM)
            seg = acc[:, sl]
            roped = seg * cos + pltpu.roll(seg, HEAD_DIM // 2, axis=1) * sin
            o_ref[:, sl] = roped.astype(o_ref.dtype)
            block_means(roped, sl)

    @pl.when(j >= n_rope_tiles)
    def _():
        o_ref[...] = acc.astype(o_ref.dtype)
        block_means(acc, slice(None))


def _rope_proj(x, w, cos, sin_signed, n_rope, seq):
    m, k = x.shape
    n = w.shape[1]
    tm = _tile(seq, 1024)
    tn = _tile(n_rope, 1024)
    assert n % tn == 0 and tm % MOBA_BLOCK == 0
    n_seq_tiles = seq // tm
    kern = functools.partial(_rope_proj_kernel, n_rope_tiles=n_rope // tn, tm=tm, tn=tn)
    return pl.pallas_call(
        kern,
        grid=(m // tm, n // tn),
        in_specs=[pl.BlockSpec((tm, k), lambda i, j: (i, 0)),
                  pl.BlockSpec((k, tn), lambda i, j: (0, j)),
                  pl.BlockSpec((tm, HEAD_DIM), lambda i, j: (i % n_seq_tiles, 0)),
                  pl.BlockSpec((tm, HEAD_DIM), lambda i, j: (i % n_seq_tiles, 0))],
        out_specs=[pl.BlockSpec((tm, tn), lambda i, j: (i, j)),
                   pl.BlockSpec((tm // MOBA_BLOCK, 1, tn), lambda i, j: (i, 0, j))],
        out_shape=[jax.ShapeDtypeStruct((m, n), BF16),
                   jax.ShapeDtypeStruct((m // MOBA_BLOCK, 1, n), F32)],
        compiler_params=_params("parallel", "arbitrary"),
        name="rope_proj",
    )(x, w, cos, sin_signed)


def _bf16_bits(v):
    return lax.bitcast_convert_type(v.astype(BF16).astype(F32), jnp.uint32)


def _pack_halves(v):
    half = v.shape[1] // 2
    return (_bf16_bits(v[:, :half]) >> 16) | _bf16_bits(v[:, half:])


def _unpack_halves(u):
    lo = lax.bitcast_convert_type(u << 16, F32).astype(BF16)
    hi = lax.bitcast_convert_type(u & jnp.uint32(0xFFFF0000), F32).astype(BF16)
    return jnp.concatenate([lo, hi], axis=1)


def _layer_norm_store(z, g_ref, b_ref, of_ref, ob_ref, op_ref=None):
    mu = jnp.mean(z, axis=-1, keepdims=True)
    zc = z - mu
    var = jnp.mean(zc * zc, axis=-1, keepdims=True)
    out = zc * lax.rsqrt(var + LN_EPS) * g_ref[...] + b_ref[...]
    of_ref[...] = out
    ob_ref[...] = out.astype(BF16)
    if op_ref is not None:
        op_ref[...] = _pack_halves(out)


def _mm_res_ln_kernel(y_ref, w_ref, x_ref, g_ref, b_ref, of_ref, ob_ref, op_ref, *, alpha):
    mix = jnp.dot(y_ref[...], w_ref[...], preferred_element_type=F32)
    _layer_norm_store(alpha * x_ref[...] + mix, g_ref, b_ref, of_ref, ob_ref, op_ref)


def _mm_res_ln(y, w, x, g, b, alpha):
    m, k = y.shape
    d = w.shape[1]
    tm = _tile(m, 256)
    row = lambda i: (i, 0)
    const = lambda i: (0, 0)
    return pl.pallas_call(
        functools.partial(_mm_res_ln_kernel, alpha=alpha),
        grid=(m // tm,),
        in_specs=[pl.BlockSpec((tm, k), row), pl.BlockSpec((k, d), const), pl.BlockSpec((tm, d), row),
                  pl.BlockSpec((1, d), const), pl.BlockSpec((1, d), const)],
        out_specs=[pl.BlockSpec((tm, d), row), pl.BlockSpec((tm, d), row), pl.BlockSpec((tm, d // 2), row)],
        out_shape=[jax.ShapeDtypeStruct((m, d), F32), jax.ShapeDtypeStruct((m, d), BF16),
                   jax.ShapeDtypeStruct((m, d // 2), jnp.uint32)],
        compiler_params=_params("parallel"),
        name="mm_res_ln",
    )(y, w, x, g.reshape(1, d), b.reshape(1, d))


def _rglru_kernel(gb_ref, u_ref, cw_ref, cb_ref, gw_ref, gbias_ref, sp_ref, y_ref,
                  uext_ref, h_ref, a_ref, b_ref, *, ts, n_blocks, cblk):
    t = pl.program_id(1)

    @pl.when(t == 0)
    def _():
        uext_ref[0:SUBLANES, :] = jnp.zeros((SUBLANES, uext_ref.shape[1]), F32)
        h_ref[...] = jnp.zeros_like(h_ref)

    uext_ref[SUBLANES:SUBLANES + ts, :] = u_ref[...]
    u = cb_ref[...] + sum(
        uext_ref[pl.ds(SUBLANES - (CONV_WIDTH - 1) + k, ts), :] * cw_ref[k:k + 1, :]
        for k in range(CONV_WIDTH))
    uext_ref[0:SUBLANES, :] = uext_ref[ts:ts + SUBLANES, :]

    for n in range(n_blocks):
        sl = slice(n * cblk, (n + 1) * cblk)
        un = u[:, sl]
        gates = jnp.dot(un.astype(BF16), gw_ref[n], preferred_element_type=F32) + gbias_ref[n]
        gates = jax.nn.sigmoid(gates)
        r = gates[:, :cblk]
        i_gate = gates[:, cblk:]
        log_a = -LRU_C * r * sp_ref[:, sl]
        a = jnp.exp(log_a)
        mult = jnp.sqrt(1.0 - jnp.exp(2.0 * log_a))
        a_ref[:, sl] = a
        b_ref[:, sl] = mult * i_gate * un

    width = a_ref.shape[1]
    row = lax.broadcasted_iota(jnp.int32, (SUBLANES, width), 0)

    def slab(s, h):
        off = pl.multiple_of(s * SUBLANES, SUBLANES)
        a = a_ref[pl.ds(off, SUBLANES), :]
        b = b_ref[pl.ds(off, SUBLANES), :]
        for sh in (1, 2, 4):
            keep = row >= sh
            a_prev = jnp.where(keep, pltpu.roll(a, sh, axis=0), 1.0)
            b_prev = jnp.where(keep, pltpu.roll(b, sh, axis=0), 0.0)
            b = a * b_prev + b
            a = a * a_prev
        hs = a * h + b
        b_ref[pl.ds(off, SUBLANES), :] = hs
        return hs[SUBLANES - 1:SUBLANES, :]

    h_ref[...] = lax.fori_loop(0, ts // SUBLANES, slab, h_ref[...])
    y_ref[...] = (jax.nn.gelu(gb_ref[...], approximate=True) * b_ref[...]).astype(y_ref.dtype)


def _rglru(proj, conv_w, conv_b, gate_w, gate_b, softplus_neg_lam, batch, seq):
    m, two_d = proj.shape
    d = two_d // 2
    n_blocks, cblk = gate_w.shape[0], gate_w.shape[1]
    ts = _tile(seq, 256)
    nt = seq // ts
    kern = functools.partial(_rglru_kernel, ts=ts, n_blocks=n_blocks, cblk=cblk)
    const2 = lambda b, t: (0, 0)
    const3 = lambda b, t: (0, 0, 0)
    return pl.pallas_call(
        kern,
        grid=(batch, nt),
        in_specs=[pl.BlockSpec((ts, d), lambda b, t: (b * nt + t, 0)),
                  pl.BlockSpec((ts, d), lambda b, t: (b * nt + t, 1)),
                  pl.BlockSpec((CONV_WIDTH, d), const2),
                  pl.BlockSpec((1, d), const2),
                  pl.BlockSpec((n_blocks, cblk, 2 * cblk), const3),
                  pl.BlockSpec((n_blocks, 1, 2 * cblk), const3),
                  pl.BlockSpec((1, d), const2)],
        out_specs=pl.BlockSpec((ts, d), lambda b, t: (b * nt + t, 0)),
        out_shape=jax.ShapeDtypeStruct((m, d), BF16),
        scratch_shapes=[pltpu.VMEM((ts + SUBLANES, d), F32), pltpu.VMEM((1, d), F32),
                        pltpu.VMEM((ts, d), F32), pltpu.VMEM((ts, d), F32)],
        compiler_params=_params("arbitrary", "arbitrary"),
        name="rglru",
    )(proj, proj, conv_w, conv_b.reshape(1, d), gate_w, gate_b.reshape(n_blocks, 1, 2 * cblk),
      softplus_neg_lam.reshape(1, d))


def _moba_kernel(q_ref, k_ref, vt_ref, km_ref, o_ref, acc_ref, *, heads, n_kv_blocks):
    i = pl.program_id(2)
    L = MOBA_BLOCK
    nt_dims = (((1,), (1,)), ((), ()))
    blk = lax.broadcasted_iota(jnp.int32, (n_kv_blocks, L), 0)
    key = lax.broadcasted_iota(jnp.int32, (L, L), 0)
    qry = lax.broadcasted_iota(jnp.int32, (L, L), 1)
    sls = [slice(h * HEAD_DIM, (h + 1) * HEAD_DIM) for h in range(heads)]
    qs = [q_ref[:, sl] for sl in sls]

    bits = []
    for h in range(heads):
        gate = lax.dot_general(km_ref[0, :, sls[h]], qs[h].astype(F32), nt_dims,
                               precision=lax.Precision.HIGHEST, preferred_element_type=F32)
        gate = jnp.where(blk < i, gate, -jnp.inf)
        chosen = jnp.zeros((n_kv_blocks, L), F32)
        for _ in range(MOBA_TOPK):
            best = jnp.max(gate, axis=0, keepdims=True)
            first = jnp.min(jnp.where(gate == best, blk, n_kv_blocks), axis=0, keepdims=True)
            pick = (blk == first) & (best > -jnp.inf)
            chosen = jnp.where(pick, 1.0, chosen)
            gate = jnp.where(pick, -jnp.inf, gate)
        weight = jnp.left_shift(1, blk).astype(F32)
        bits.append(jnp.sum(chosen * weight, axis=0, keepdims=True).astype(jnp.int32))
        acc_ref[h] = jnp.zeros((HEAD_DIM, L), F32)

    def scores(koff):
        return [lax.dot_general(k_ref[pl.ds(koff, L), sls[h]], qs[h], nt_dims, preferred_element_type=F32)
                for h in range(heads)]

    def accumulate(jblk, ps):
        for h in range(heads):
            acc_ref[h] += jnp.dot(vt_ref[0, jblk, sls[h], :], ps[h], preferred_element_type=F32)

    def past(j, carry):
        ss = scores(pl.multiple_of(j * L, L))
        ps, out = [], []
        for h in range(heads):
            m, l = carry[h]
            picked = (jnp.right_shift(bits[h], j) & 1) > 0
            m_new = jnp.where(picked, jnp.maximum(m, jnp.max(ss[h], axis=0, keepdims=True)), m)
            alpha = jnp.exp2(m - m_new)
            p = jnp.exp2(ss[h] - jnp.where(picked, m_new, -NEG))
            l = alpha * l + jnp.sum(p, axis=0, keepdims=True)
            acc_ref[h] = alpha * acc_ref[h]
            ps.append(p.astype(BF16))
            out.append((m_new, l))
        accumulate(j, ps)
        return tuple(out)

    def own(carry):
        ss = scores(pl.multiple_of(i * L, L))
        ps, ls = [], []
        for h in range(heads):
            m, l = carry[h]
            s = jnp.where(key <= qry, ss[h], NEG)
            m_new = jnp.maximum(m, jnp.max(s, axis=0, keepdims=True))
            alpha = jnp.exp2(m - m_new)
            p = jnp.exp2(s - m_new)
            ls.append(alpha * l + jnp.sum(p, axis=0, keepdims=True))
            acc_ref[h] = alpha * acc_ref[h]
            ps.append(p.astype(BF16))
        accumulate(i, ps)
        return ls

    init = tuple((jnp.full((1, L), NEG, F32), jnp.zeros((1, L), F32)) for _ in range(heads))
    ls = own(lax.fori_loop(0, i, past, init))
    for h in range(heads):
        o_ref[:, sls[h]] = (acc_ref[h] / ls[h]).T.astype(o_ref.dtype)


def _moba(q, kv, vt, k_mean, batch, seq):
    m, d_attn = q.shape
    heads = min(8, d_attn // HEAD_DIM)
    gw = heads * HEAD_DIM
    n_groups = d_attn // gw
    nq = seq // MOBA_BLOCK
    kern = functools.partial(_moba_kernel, heads=heads, n_kv_blocks=nq)
    return pl.pallas_call(
        kern,
        grid=(batch, n_groups, nq),
        in_specs=[pl.BlockSpec((MOBA_BLOCK, gw), lambda b, g, i: (b * nq + i, g)),
                  pl.BlockSpec((seq, gw), lambda b, g, i: (b, g)),
                  pl.BlockSpec((1, nq, gw, MOBA_BLOCK), lambda b, g, i: (b, 0, g, 0)),
                  pl.BlockSpec((1, nq, gw), lambda b, g, i: (b, 0, g))],
        out_specs=pl.BlockSpec((MOBA_BLOCK, gw), lambda b, g, i: (b * nq + i, g)),
        out_shape=jax.ShapeDtypeStruct((m, d_attn), BF16),
        scratch_shapes=[pltpu.VMEM((heads, HEAD_DIM, MOBA_BLOCK), F32)],
        compiler_params=_params("parallel", "parallel", "arbitrary"),
        name="moba",
    )(q, kv, vt, k_mean)


def _router_kernel(x_ref, wt_ref, bias_ref, tope_ref, w8_ref, rank8_ref, counts_ref, carry_ref,
                   *, tm, n_experts):
    @pl.when(pl.program_id(0) == 0)
    def _():
        carry_ref[...] = jnp.zeros_like(carry_ref)

    nt_dims = (((1,), (1,)), ((), ()))
    hi_lo = lambda v: (v.astype(BF16), (v - v.astype(BF16).astype(F32)).astype(BF16))
    xh, xl = hi_lo(x_ref[...])
    wh, wl = hi_lo(wt_ref[...])
    ntdot = lambda a, b: lax.dot_general(a, b, nt_dims, preferred_element_type=F32)
    logits = ntdot(wh, xh) + (ntdot(wl, xh) + ntdot(wh, xl))
    scores = jax.nn.sigmoid(logits)
    sel = scores + bias_ref[...]

    gsz = n_experts // N_GROUPS
    cmax = lambda v: jnp.max(v, axis=0, keepdims=True)
    csum = lambda v: jnp.sum(v, axis=0, keepdims=True)
    first_of = lambda v, m, idx, n: jnp.min(jnp.where(v == m, idx, float(n)), axis=0, keepdims=True)

    in_group = lax.broadcasted_iota(jnp.int32, (gsz, tm), 0).astype(F32)
    groups = [sel[g * gsz:(g + 1) * gsz, :] for g in range(N_GROUPS)]
    gscore = []
    for sg in groups:
        m1 = cmax(sg)
        m2 = cmax(jnp.where(in_group == first_of(sg, m1, in_group, gsz), -jnp.inf, sg))
        gscore.append(m1 + m2)
    gidx = lax.broadcasted_iota(jnp.int32, (N_GROUPS, tm), 0)
    gs = jnp.concatenate(gscore, axis=0)
    beaten = jnp.zeros((N_GROUPS, tm), jnp.int32)
    for o in range(N_GROUPS):
        ahead = (gscore[o] > gs) | ((gscore[o] == gs) & (o < gidx))
        beaten = beaten + ahead.astype(jnp.int32)
    keep = beaten < TOPK_GROUPS
    cand = jnp.concatenate([jnp.where(keep[g:g + 1, :], groups[g], -jnp.inf) for g in range(N_GROUPS)], axis=0)

    eidx = lax.broadcasted_iota(jnp.int32, (n_experts, tm), 0).astype(F32)
    chosen = jnp.zeros((n_experts, tm), F32)
    picks, tope_rows, score_rows = [], [], []
    for _ in range(TOP_K):
        idx = first_of(cand, cmax(cand), eidx, n_experts)
        hit = eidx == idx
        picks.append(hit)
        tope_rows.append(idx)
        score_rows.append(csum(jnp.where(hit, scores, 0.0)))
        chosen = jnp.where(hit, 1.0, chosen)
        cand = jnp.where(hit, -jnp.inf, cand)
    tope_ref[...] = jnp.concatenate(tope_rows, axis=0).astype(jnp.int32)
    s8 = jnp.concatenate(score_rows, axis=0)
    w8_ref[...] = s8 / csum(s8) * ROUTED_SCALE

    r = lax.broadcasted_iota(jnp.int32, (tm, tm), 0)
    c = lax.broadcasted_iota(jnp.int32, (tm, tm), 1)
    before = jnp.dot(chosen.astype(BF16), (r < c).astype(BF16), preferred_element_type=F32) + carry_ref[...]
    rank_rows = [csum(jnp.where(picks[k], before, 0.0)) for k in range(TOP_K)]
    rank8_ref[...] = jnp.concatenate(rank_rows, axis=0).astype(jnp.int32)
    carry_ref[...] += jnp.sum(chosen, axis=1, keepdims=True)
    counts_ref[...] = carry_ref[...].astype(jnp.int32)


def _router(x, w, bias):
    m, d = x.shape
    e = w.shape[1]
    tm = _tile(m, 256)
    const = lambda i: (0, 0)
    col = lambda i: (0, i)
    return pl.pallas_call(
        functools.partial(_router_kernel, tm=tm, n_experts=e),
        grid=(m // tm,),
        in_specs=[pl.BlockSpec((tm, d), lambda i: (i, 0)), pl.BlockSpec((e, d), const), pl.BlockSpec((e, 1), const)],
        out_specs=[pl.BlockSpec((TOP_K, tm), col), pl.BlockSpec((TOP_K, tm), col), pl.BlockSpec((TOP_K, tm), col),
                   pl.BlockSpec((e, 1), const)],
        out_shape=[jax.ShapeDtypeStruct((TOP_K, m), jnp.int32), jax.ShapeDtypeStruct((TOP_K, m), F32),
                   jax.ShapeDtypeStruct((TOP_K, m), jnp.int32), jax.ShapeDtypeStruct((e, 1), jnp.int32)],
        scratch_shapes=[pltpu.VMEM((e, 1), F32)],
        compiler_params=_params("arbitrary"),
        name="router",
    )(x, w.astype(F32).T, bias.astype(F32).reshape(e, 1))


def _dispatch_kernel(dest_ref, x_ref, init_ref, o_ref, sem, *, tm):
    del init_ref

    def send(r, carry):
        for k in range(TOP_K):
            d = dest_ref[0, 0, r * TOP_K + k]
            pltpu.make_async_copy(x_ref.at[pl.ds(r, 1), :], o_ref.at[pl.ds(d, 1), :], sem).start()
        return carry

    lax.fori_loop(0, tm, send, 0)
    for _ in range(TOP_K):
        pltpu.make_async_copy(x_ref, o_ref.at[pl.ds(0, tm), :], sem).wait()


def _dispatch(x, dest, init):
    n, d = x.shape
    tm = _tile(n, 256)
    assert init.shape[1] == d and init.dtype == x.dtype
    return pl.pallas_call(
        functools.partial(_dispatch_kernel, tm=tm),
        grid=(n // tm,),
        in_specs=[pl.BlockSpec((1, 1, tm * TOP_K), lambda i: (i, 0, 0), memory_space=pltpu.SMEM),
                  pl.BlockSpec((tm, d), lambda i: (i, 0)),
                  pl.BlockSpec(memory_space=pl.ANY)],
        out_specs=pl.BlockSpec(memory_space=pl.ANY),
        out_shape=jax.ShapeDtypeStruct(init.shape, x.dtype),
        scratch_shapes=[pltpu.SemaphoreType.DMA(())],
        input_output_aliases={2: 0},
        compiler_params=_params("arbitrary"),
        name="dispatch",
    )(dest.reshape(n // tm, 1, tm * TOP_K), x, init)


def _experts_kernel(be_ref, nxt_ref, slot_ref, nv_ref, x_ref, wgu_hbm, wdn_hbm, o_ref,
                    wgu_f32, wdn_f32, wgu_bf, wdn_bf, sems, *, d_expert, layer):
    i = pl.program_id(0)
    e = be_ref[i]
    slot = slot_ref[i]
    live = i < nv_ref[0]

    def weight_copies(expert, s):
        return (pltpu.make_async_copy(wgu_hbm.at[layer, expert], wgu_f32.at[s], sems.at[0, s]),
                pltpu.make_async_copy(wdn_hbm.at[layer, expert], wdn_f32.at[s], sems.at[1, s]))

    @pl.when(live & (i == 0))
    def _():
        for cp in weight_copies(e, slot):
            cp.start()

    @pl.when(live & ((i == 0) | (e != be_ref[jnp.maximum(i - 1, 0)])))
    def _():
        for cp in weight_copies(e, slot):
            cp.wait()
        nxt = nxt_ref[i]

        @pl.when(nxt >= 0)
        def _():
            for cp in weight_copies(nxt, 1 - slot):
                cp.start()

        wgu_bf[...] = wgu_f32[slot].astype(BF16)
        wdn_bf[...] = wdn_f32[slot].astype(BF16)

    @pl.when(live)
    def _():
        h = jnp.dot(_unpack_halves(x_ref[...]), wgu_bf[...], preferred_element_type=F32)
        act = jax.nn.silu(h[:, :d_expert]) * h[:, d_expert:]
        o_ref[...] = jnp.dot(act.astype(BF16), wdn_bf[...], preferred_element_type=F32).astype(o_ref.dtype)

    @pl.when(jnp.logical_not(live))
    def _():
        o_ref[...] = jnp.zeros_like(o_ref)


def _experts(x_rows, block_e, next_e, slot, n_valid, w_gu, w_down, layer):
    p, half = x_rows.shape
    d = w_gu.shape[2]
    assert d == 2 * half and x_rows.dtype == jnp.uint32
    d_expert = w_down.shape[2]
    tm = EXPERT_ROW_BLOCK
    grid_spec = pltpu.PrefetchScalarGridSpec(
        num_scalar_prefetch=4,
        grid=(p // tm,),
        in_specs=[pl.BlockSpec((tm, half), lambda i, be, nx, sl, nv: (jnp.maximum(jnp.minimum(i, nv[0] - 1), 0), 0)),
                  pl.BlockSpec(memory_space=pl.ANY),
                  pl.BlockSpec(memory_space=pl.ANY)],
        out_specs=pl.BlockSpec((tm, d), lambda i, be, nx, sl, nv: (i, 0)),
        scratch_shapes=[pltpu.VMEM((2, d, 2 * d_expert), F32), pltpu.VMEM((2, d_expert, d), F32),
                        pltpu.VMEM((d, 2 * d_expert), BF16), pltpu.VMEM((d_expert, d), BF16),
                        pltpu.SemaphoreType.DMA((2, 2))],
    )
    return pl.pallas_call(
        functools.partial(_experts_kernel, d_expert=d_expert, layer=layer),
        grid_spec=grid_spec,
        out_shape=jax.ShapeDtypeStruct((p, d), BF16),
        compiler_params=_params("arbitrary"),
        name="experts",
    )(block_e, next_e, slot, n_valid, x_rows, w_gu, w_down)


def _shared_ln_kernel(xb_ref, x_ref, yg_ref, w8_ref, wgu_ref, wdn_ref, g_ref, b_ref, of_ref, ob_ref,
                      *, alpha, d_expert):
    h = jnp.dot(xb_ref[...], wgu_ref[...], preferred_element_type=F32)
    act = jax.nn.silu(h[:, :d_expert]) * h[:, d_expert:]
    ffn = jnp.dot(act.astype(BF16), wdn_ref[...], preferred_element_type=F32)
    w8 = w8_ref[...]
    for k in range(TOP_K):
        ffn = ffn + w8[:, k:k + 1] * yg_ref[k].astype(F32)
    _layer_norm_store(alpha * x_ref[...] + ffn, g_ref, b_ref, of_ref, ob_ref)


def _shared_ln(xb, x, yg, w8, w_gu, w_down, g, b, alpha):
    m, d = x.shape
    d_expert = w_down.shape[0]
    tm = _tile(m, 256)
    row = lambda i: (i, 0)
    const = lambda i: (0, 0)
    return pl.pallas_call(
        functools.partial(_shared_ln_kernel, alpha=alpha, d_expert=d_expert),
        grid=(m // tm,),
        in_specs=[pl.BlockSpec((tm, d), row), pl.BlockSpec((tm, d), row),
                  pl.BlockSpec((TOP_K, tm, d), lambda i: (0, i, 0)),
                  pl.BlockSpec((tm, TOP_K), row),
                  pl.BlockSpec((d, 2 * d_expert), const), pl.BlockSpec((d_expert, d), const),
                  pl.BlockSpec((1, d), const), pl.BlockSpec((1, d), const)],
        out_specs=[pl.BlockSpec((tm, d), row), pl.BlockSpec((tm, d), row)],
        out_shape=[jax.ShapeDtypeStruct((m, d), F32), jax.ShapeDtypeStruct((m, d), BF16)],
        compiler_params=_params("parallel"),
        name="shared_ln",
    )(xb, x, yg, w8, w_gu, w_down, g.reshape(1, d), b.reshape(1, d))


def _moe(x, xb, xp, router_w, router_bias, w_gu, w_down, layer, sh_gu, sh_down, g, b, alpha, rows_buf):
    n, d = x.shape
    n_experts = router_w.shape[1]
    tm = EXPERT_ROW_BLOCK
    top_e, w8, rank8, counts = _router(x, router_w, router_bias)

    counts = counts[:, 0]
    experts = jnp.arange(n_experts, dtype=jnp.int32)
    padded = (counts + tm - 1) // tm * tm
    pend = jnp.cumsum(padded)
    pstart = pend - padded
    dest = rank8 + jnp.sum(jnp.where(top_e[..., None] == experts, pstart, 0), axis=-1)
    n_blocks = (n * TOP_K + n_experts * (tm - 1)) // tm + 1
    n_valid = (pend[-1] // tm).astype(jnp.int32)
    blk_start = jnp.minimum(jnp.arange(n_blocks, dtype=jnp.int32), n_valid - 1) * tm
    block_e = jnp.minimum(jnp.sum((pend[None, :] <= blk_start[:, None]).astype(jnp.int32), axis=1), n_experts - 1)
    nonempty = counts > 0
    later = (experts[None, :] > block_e[:, None]) & nonempty[None, :]
    next_e = jnp.min(jnp.where(later, experts[None, :], n_experts), axis=1)
    next_e = jnp.where(next_e == n_experts, -1, next_e).astype(jnp.int32)
    earlier = (experts[None, :] < block_e[:, None]) & nonempty[None, :]
    slot = (jnp.sum(earlier.astype(jnp.int32), axis=1) % 2).astype(jnp.int32)

    if rows_buf is None:
        rows_buf = jnp.zeros((n_blocks * tm, xp.shape[1]), xp.dtype)
    x_rows = _dispatch(xp, dest.T, rows_buf)
    y_rows = _experts(x_rows, block_e, next_e, slot, n_valid.reshape(1), w_gu, w_down, layer)
    yg = y_rows[dest.reshape(-1)].reshape(TOP_K, n, d)
    xf, xb = _shared_ln(xb, x, yg, w8.T, sh_gu, sh_down, g, b, alpha)
    return xf, xb, x_rows


def kernel(x, ln_mix_g, ln_mix_b, ln_ffn_g, ln_ffn_b, a_w_in, a_conv_w, a_conv_b, a_gate_w, a_gate_b, a_lambda, a_w_out, kv_w, b_w_q, b_w_o, router_w, router_bias, moe_w_gu, moe_w_down, sh_w_gu, sh_w_down):
    batch, seq, d = x.shape
    n = batch * seq
    n_a = a_w_in.shape[0]
    n_b = b_w_q.shape[0]
    depth = n_a + n_b
    alpha = (2.0 * depth) ** 0.25
    d_attn = b_w_q.shape[2]

    inv = ROPE_THETA ** (-jnp.arange(0, HEAD_DIM, 2, dtype=F32) / HEAD_DIM)
    ang = jnp.arange(seq, dtype=F32)[:, None] * inv[None, :]
    ang = jnp.concatenate([ang, ang], axis=-1)
    cos = jnp.cos(ang)
    sign = jnp.where(jnp.arange(HEAD_DIM) < HEAD_DIM // 2, -1.0, 1.0).astype(F32)
    sin_signed = jnp.sin(ang) * sign
    q_scale = HEAD_DIM ** -0.5 * math.log2(math.e)

    xf = x.reshape(n, d)
    xb = xf.astype(BF16)
    kv = k_mean = v_t = rows_buf = None
    for layer in range(depth):
        if layer < n_a:
            proj = _proj(xb, a_w_in[layer].astype(BF16), F32)
            y = _rglru(proj, a_conv_w[layer], a_conv_b[layer], a_gate_w[layer].astype(BF16), a_gate_b[layer],
                       jax.nn.softplus(-a_lambda[layer].astype(F32)), batch, seq)
            w_o = a_w_out[layer]
        else:
            j = layer - n_a
            if j == 0:
                kv, k_mean = _rope_proj(xb, kv_w.astype(BF16), cos, sin_signed, d_attn, seq)
                k_mean = k_mean.reshape(batch, seq // MOBA_BLOCK, 2 * d_attn)
                v_t = jnp.swapaxes(kv[:, d_attn:].reshape(batch, seq // MOBA_BLOCK, MOBA_BLOCK, d_attn), 2, 3)
            q, _ = _rope_proj(xb, b_w_q[j].astype(BF16), cos * q_scale, sin_signed * q_scale, d_attn, seq)
            y = _moba(q, kv, v_t, k_mean, batch, seq)
            w_o = b_w_o[j]
        xf, xb, xp = _mm_res_ln(y, w_o.astype(BF16), xf, ln_mix_g[layer], ln_mix_b[layer], alpha)
        xf, xb, rows_buf = _moe(xf, xb, xp, router_w[layer], router_bias[layer], moe_w_gu, moe_w_down, layer,
                                sh_w_gu[layer].astype(BF16), sh_w_down[layer].astype(BF16),
                                ln_ffn_g[layer], ln_ffn_b[layer], alpha, rows_buf)
    return xf.reshape(batch, seq, d)
```

```python
import functools
import math

import jax
import jax.numpy as jnp
from jax import lax
from jax.experimental import pallas as pl
from jax.experimental.pallas import tpu as pltpu

HEAD_DIM = 128
MOBA_BLOCK = 256
MOBA_TOPK = 3
ROPE_THETA = 10000.0
CONV_WIDTH = 4
LRU_C = 8.0
TOP_K = 8
N_GROUPS = 8
TOPK_GROUPS = 4
ROUTED_SCALE = 2.5
LN_EPS = 1e-5
NEG = -1e30

VMEM_LIMIT_BYTES = 56 * 1024 * 1024
SUBLANES = 8
EXPERT_ROW_BLOCK = 256

F32 = jnp.float32
BF16 = jnp.bfloat16


def _params(*sem):
    return pltpu.CompilerParams(dimension_semantics=sem, vmem_limit_bytes=VMEM_LIMIT_BYTES)


def _tile(n, pref):
    t = min(n, pref)
    assert n % t == 0, (n, t)
    return t


def _proj_kernel(x_ref, w_ref, o_ref):
    o_ref[...] = jnp.dot(x_ref[...], w_ref[...], preferred_element_type=F32).astype(o_ref.dtype)


def _proj(x, w, out_dtype):
    m, k = x.shape
    n = w.shape[1]
    tm, tn = _tile(m, 1024), _tile(n, 1024)
    return pl.pallas_call(
        _proj_kernel,
        grid=(m // tm, n // tn),
        in_specs=[pl.BlockSpec((tm, k), lambda i, j: (i, 0)),
                  pl.BlockSpec((k, tn), lambda i, j: (0, j))],
        out_specs=pl.BlockSpec((tm, tn), lambda i, j: (i, j)),
        out_shape=jax.ShapeDtypeStruct((m, n), out_dtype),
        compiler_params=_params("parallel", "arbitrary"),
        name="proj",
    )(x, w)


def _rope_proj_kernel(x_ref, w_ref, cos_ref, sin_ref, o_ref, km_ref, *, n_rope_tiles, tm, tn):
    acc = jnp.dot(x_ref[...], w_ref[...], preferred_element_type=F32)
    j = pl.program_id(1)
    nb = tm // MOBA_BLOCK

    def block_means(val, sl):
        for r in range(nb):
            km_ref[r, :, sl] = jnp.mean(val[r * MOBA_BLOCK:(r + 1) * MOBA_BLOCK], axis=0, keepdims=True)

    @pl.when(j < n_rope_tiles)
    def _():
        cos = cos_ref[...]
        sin = sin_ref[...]
        for h in range(tn // HEAD_DIM):
            sl = slice(h * HEAD_DIM, (h + 1) * HEAD_DIM)
            seg = acc[:, sl]
            roped = seg * cos + pltpu.roll(seg, HEAD_DIM // 2, axis=1) * sin
            o_ref[:, sl] = roped.astype(o_ref.dtype)
            block_means(roped, sl)

    @pl.when(j >= n_rope_tiles)
    def _():
        o_ref[...] = acc.astype(o_ref.dtype)
        block_means(acc, slice(None))


def _rope_proj(x, w, cos, sin_signed, n_rope, seq):
    m, k = x.shape
    n = w.shape[1]
    tm = _tile(seq, 1024)
    tn = _tile(n_rope, 1024)
    assert n % tn == 0 and tm % MOBA_BLOCK == 0
    n_seq_tiles = seq // tm
    kern = functools.partial(_rope_proj_kernel, n_rope_tiles=n_rope // tn, tm=tm, tn=tn)
    return pl.pallas_call(
        kern,
        grid=(m // tm, n // tn),
        in_specs=[pl.BlockSpec((tm, k), lambda i, j: (i, 0)),
                  pl.BlockSpec((k, tn), lambda i, j: (0, j)),
                  pl.BlockSpec((tm, HEAD_DIM), lambda i, j: (i % n_seq_tiles, 0)),
                  pl.BlockSpec((tm, HEAD_DIM), lambda i, j: (i % n_seq_tiles, 0))],
        out_specs=[pl.BlockSpec((tm, tn), lambda i, j: (i, j)),
                   pl.BlockSpec((tm // MOBA_BLOCK, 1, tn), lambda i, j: (i, 0, j))],
        out_shape=[jax.ShapeDtypeStruct((m, n), BF16),
                   jax.ShapeDtypeStruct((m // MOBA_BLOCK, 1, n), F32)],
        compiler_params=_params("parallel", "arbitrary"),
        name="rope_proj",
    )(x, w, cos, sin_signed)


def _bf16_bits(v):
    return lax.bitcast_convert_type(v.astype(BF16).astype(F32), jnp.uint32)


def _pack_halves(v):
    half = v.shape[1] // 2
    return (_bf16_bits(v[:, :half]) >> 16) | _bf16_bits(v[:, half:])


def _unpack_halves(u):
    lo = lax.bitcast_convert_type(u << 16, F32).astype(BF16)
    hi = lax.bitcast_convert_type(u & jnp.uint32(0xFFFF0000), F32).astype(BF16)
    return jnp.concatenate([lo, hi], axis=1)


def _layer_norm_store(z, g_ref, b_ref, of_ref, ob_ref, op_ref=None):
    mu = jnp.mean(z, axis=-1, keepdims=True)
    zc = z - mu
    var = jnp.mean(zc * zc, axis=-1, keepdims=True)
    out = zc * lax.rsqrt(var + LN_EPS) * g_ref[...] + b_ref[...]
    of_ref[...] = out
    ob_ref[...] = out.astype(BF16)
    if op_ref is not None:
        op_ref[...] = _pack_halves(out)


def _mm_res_ln_kernel(y_ref, w_ref, x_ref, g_ref, b_ref, of_ref, ob_ref, op_ref, *, alpha):
    mix = jnp.dot(y_ref[...], w_ref[...], preferred_element_type=F32)
    _layer_norm_store(alpha * x_ref[...] + mix, g_ref, b_ref, of_ref, ob_ref, op_ref)


def _mm_res_ln(y, w, x, g, b, alpha):
    m, k = y.shape
    d = w.shape[1]
    tm = _tile(m, 256)
    row = lambda i: (i, 0)
    const = lambda i: (0, 0)
    return pl.pallas_call(
        functools.partial(_mm_res_ln_kernel, alpha=alpha),
        grid=(m // tm,),
        in_specs=[pl.BlockSpec((tm, k), row), pl.BlockSpec((k, d), const), pl.BlockSpec((tm, d), row),
                  pl.BlockSpec((1, d), const), pl.BlockSpec((1, d), const)],
        out_specs=[pl.BlockSpec((tm, d), row), pl.BlockSpec((tm, d), row), pl.BlockSpec((tm, d // 2), row)],
        out_shape=[jax.ShapeDtypeStruct((m, d), F32), jax.ShapeDtypeStruct((m, d), BF16),
                   jax.ShapeDtypeStruct((m, d // 2), jnp.uint32)],
        compiler_params=_params("parallel"),
        name="mm_res_ln",
    )(y, w, x, g.reshape(1, d), b.reshape(1, d))


def _rglru_kernel(gb_ref, u_ref, cw_ref, cb_ref, gw_ref, gbias_ref, sp_ref, y_ref,
                  uext_ref, h_ref, a_ref, b_ref, *, ts, n_blocks, cblk):
    t = pl.program_id(1)

    @pl.when(t == 0)
    def _():
        uext_ref[0:SUBLANES, :] = jnp.zeros((SUBLANES, uext_ref.shape[1]), F32)
        h_ref[...] = jnp.zeros_like(h_ref)

    uext_ref[SUBLANES:SUBLANES + ts, :] = u_ref[...]
    u = cb_ref[...] + sum(
        uext_ref[pl.ds(SUBLANES - (CONV_WIDTH - 1) + k, ts), :] * cw_ref[k:k + 1, :]
        for k in range(CONV_WIDTH))
    uext_ref[0:SUBLANES, :] = uext_ref[ts:ts + SUBLANES, :]

    for n in range(n_blocks):
        sl = slice(n * cblk, (n + 1) * cblk)
        un = u[:, sl]
        gates = jnp.dot(un.astype(BF16), gw_ref[n], preferred_element_type=F32) + gbias_ref[n]
        gates = jax.nn.sigmoid(gates)
        r = gates[:, :cblk]
        i_gate = gates[:, cblk:]
        log_a = -LRU_C * r * sp_ref[:, sl]
        a = jnp.exp(log_a)
        mult = jnp.sqrt(1.0 - jnp.exp(2.0 * log_a))
        a_ref[:, sl] = a
        b_ref[:, sl] = mult * i_gate * un

    width = a_ref.shape[1]
    row = lax.broadcasted_iota(jnp.int32, (SUBLANES, width), 0)

    def slab(s, h):
        off = pl.multiple_of(s * SUBLANES, SUBLANES)
        a = a_ref[pl.ds(off, SUBLANES), :]
        b = b_ref[pl.ds(off, SUBLANES), :]
        for sh in (1, 2, 4):
            keep = row >= sh
            a_prev = jnp.where(keep, pltpu.roll(a, sh, axis=0), 1.0)
            b_prev = jnp.where(keep, pltpu.roll(b, sh, axis=0), 0.0)
            b = a * b_prev + b
            a = a * a_prev
        hs = a * h + b
        b_ref[pl.ds(off, SUBLANES), :] = hs
        return hs[SUBLANES - 1:SUBLANES, :]

    h_ref[...] = lax.fori_loop(0, ts // SUBLANES, slab, h_ref[...])
    y_ref[...] = (jax.nn.gelu(gb_ref[...], approximate=True) * b_ref[...]).astype(y_ref.dtype)


def _rglru(proj, conv_w, conv_b, gate_w, gate_b, softplus_neg_lam, batch, seq):
    m, two_d = proj.shape
    d = two_d // 2
    n_blocks, cblk = gate_w.shape[0], gate_w.shape[1]
    ts = _tile(seq, 256)
    nt = seq // ts
    kern = functools.partial(_rglru_kernel, ts=ts, n_blocks=n_blocks, cblk=cblk)
    const2 = lambda b, t: (0, 0)
    const3 = lambda b, t: (0, 0, 0)
    return pl.pallas_call(
        kern,
        grid=(batch, nt),
        in_specs=[pl.BlockSpec((ts, d), lambda b, t: (b * nt + t, 0)),
                  pl.BlockSpec((ts, d), lambda b, t: (b * nt + t, 1)),
                  pl.BlockSpec((CONV_WIDTH, d), const2),
                  pl.BlockSpec((1, d), const2),
                  pl.BlockSpec((n_blocks, cblk, 2 * cblk), const3),
                  pl.BlockSpec((n_blocks, 1, 2 * cblk), const3),
                  pl.BlockSpec((1, d), const2)],
        out_specs=pl.BlockSpec((ts, d), lambda b, t: (b * nt + t, 0)),
        out_shape=jax.ShapeDtypeStruct((m, d), BF16),
        scratch_shapes=[pltpu.VMEM((ts + SUBLANES, d), F32), pltpu.VMEM((1, d), F32),
                        pltpu.VMEM((ts, d), F32), pltpu.VMEM((ts, d), F32)],
        compiler_params=_params("arbitrary", "arbitrary"),
        name="rglru",
    )(proj, proj, conv_w, conv_b.reshape(1, d), gate_w, gate_b.reshape(n_blocks, 1, 2 * cblk),
      softplus_neg_lam.reshape(1, d))


def _moba_kernel(q_ref, k_ref, vt_ref, km_ref, o_ref, acc_ref, *, heads, n_kv_blocks):
    i = pl.program_id(2)
    L = MOBA_BLOCK
    nt_dims = (((1,), (1,)), ((), ()))
    blk = lax.broadcasted_iota(jnp.int32, (n_kv_blocks, L), 0)
    key = lax.broadcasted_iota(jnp.int32, (L, L), 0)
    qry = lax.broadcasted_iota(jnp.int32, (L, L), 1)
    sls = [slice(h * HEAD_DIM, (h + 1) * HEAD_DIM) for h in range(heads)]
    qs = [q_ref[:, sl] for sl in sls]

    bits = []
    for h in range(heads):
        gate = lax.dot_general(km_ref[0, :, sls[h]], qs[h].astype(F32), nt_dims,
                               precision=lax.Precision.HIGHEST, preferred_element_type=F32)
        gate = jnp.where(blk < i, gate, -jnp.inf)
        chosen = jnp.zeros((n_kv_blocks, L), F32)
        for _ in range(MOBA_TOPK):
            best = jnp.max(gate, axis=0, keepdims=True)
            first = jnp.min(jnp.where(gate == best, blk, n_kv_blocks), axis=0, keepdims=True)
            pick = (blk == first) & (best > -jnp.inf)
            chosen = jnp.where(pick, 1.0, chosen)
            gate = jnp.where(pick, -jnp.inf, gate)
        weight = jnp.left_shift(1, blk).astype(F32)
        bits.append(jnp.sum(chosen * weight, axis=0, keepdims=True).astype(jnp.int32))
        acc_ref[h] = jnp.zeros((HEAD_DIM, L), F32)

    def scores(koff):
        return [lax.dot_general(k_ref[pl.ds(koff, L), sls[h]], qs[h], nt_dims, preferred_element_type=F32)
                for h in range(heads)]

    def accumulate(jblk, ps):
        for h in range(heads):
            acc_ref[h] += jnp.dot(vt_ref[0, jblk, sls[h], :], ps[h], preferred_element_type=F32)

    def past(j, carry):
        ss = scores(pl.multiple_of(j * L, L))
        ps, out = [], []
        for h in range(heads):
            m, l = carry[h]
            picked = (jnp.right_shift(bits[h], j) & 1) > 0
            m_new = jnp.where(picked, jnp.maximum(m, jnp.max(ss[h], axis=0, keepdims=True)), m)
            alpha = jnp.exp2(m - m_new)
            p = jnp.exp2(ss[h] - jnp.where(picked, m_new, -NEG))
            l = alpha * l + jnp.sum(p, axis=0, keepdims=True)
            acc_ref[h] = alpha * acc_ref[h]
            ps.append(p.astype(BF16))
            out.append((m_new, l))
        accumulate(j, ps)
        return tuple(out)

    def own(carry):
        ss = scores(pl.multiple_of(i * L, L))
        ps, ls = [], []
        for h in range(heads):
            m, l = carry[h]
            s = jnp.where(key <= qry, ss[h], NEG)
            m_new = jnp.maximum(m, jnp.max(s, axis=0, keepdims=True))
            alpha = jnp.exp2(m - m_new)
            p = jnp.exp2(s - m_new)
            ls.append(alpha * l + jnp.sum(p, axis=0, keepdims=True))
            acc_ref[h] = alpha * acc_ref[h]
            ps.append(p.astype(BF16))
        accumulate(i, ps)
        return ls

    init = tuple((jnp.full((1, L), NEG, F32), jnp.zeros((1, L), F32)) for _ in range(heads))
    ls = own(lax.fori_loop(0, i, past, init))
    for h in range(heads):
        o_ref[:, sls[h]] = (acc_ref[h] / ls[h]).T.astype(o_ref.dtype)


def _moba(q, kv, vt, k_mean, batch, seq):
    m, d_attn = q.shape
    heads = min(8, d_attn // HEAD_DIM)
    gw = heads * HEAD_DIM
    n_groups = d_attn // gw
    nq = seq // MOBA_BLOCK
    kern = functools.partial(_moba_kernel, heads=heads, n_kv_blocks=nq)
    return pl.pallas_call(
        kern,
        grid=(batch, n_groups, nq),
        in_specs=[pl.BlockSpec((MOBA_BLOCK, gw), lambda b, g, i: (b * nq + i, g)),
                  pl.BlockSpec((seq, gw), lambda b, g, i: (b, g)),
                  pl.BlockSpec((1, nq, gw, MOBA_BLOCK), lambda b, g, i: (b, 0, g, 0)),
                  pl.BlockSpec((1, nq, gw), lambda b, g, i: (b, 0, g))],
        out_specs=pl.BlockSpec((MOBA_BLOCK, gw), lambda b, g, i: (b * nq + i, g)),
        out_shape=jax.ShapeDtypeStruct((m, d_attn), BF16),
        scratch_shapes=[pltpu.VMEM((heads, HEAD_DIM, MOBA_BLOCK), F32)],
        compiler_params=_params("parallel", "parallel", "arbitrary"),
        name="moba",
    )(q, kv, vt, k_mean)


def _router_kernel(x_ref, wt_ref, bias_ref, tope_ref, w8_ref, rank8_ref, counts_ref, carry_ref,
                   *, tm, n_experts):
    @pl.when(pl.program_id(0) == 0)
    def _():
        carry_ref[...] = jnp.zeros_like(carry_ref)

    nt_dims = (((1,), (1,)), ((), ()))
    hi_lo = lambda v: (v.astype(BF16), (v - v.astype(BF16).astype(F32)).astype(BF16))
    xh, xl = hi_lo(x_ref[...])
    wh, wl = hi_lo(wt_ref[...])
    ntdot = lambda a, b: lax.dot_general(a, b, nt_dims, preferred_element_type=F32)
    logits = ntdot(wh, xh) + (ntdot(wl, xh) + ntdot(wh, xl))
    scores = jax.nn.sigmoid(logits)
    sel = scores + bias_ref[...]

    gsz = n_experts // N_GROUPS
    cmax = lambda v: jnp.max(v, axis=0, keepdims=True)
    csum = lambda v: jnp.sum(v, axis=0, keepdims=True)
    first_of = lambda v, m, idx, n: jnp.min(jnp.where(v == m, idx, float(n)), axis=0, keepdims=True)

    in_group = lax.broadcasted_iota(jnp.int32, (gsz, tm), 0).astype(F32)
    groups = [sel[g * gsz:(g + 1) * gsz, :] for g in range(N_GROUPS)]
    gscore = []
    for sg in groups:
        m1 = cmax(sg)
        m2 = cmax(jnp.where(in_group == first_of(sg, m1, in_group, gsz), -jnp.inf, sg))
        gscore.append(m1 + m2)
    gidx = lax.broadcasted_iota(jnp.int32, (N_GROUPS, tm), 0)
    gs = jnp.concatenate(gscore, axis=0)
    beaten = jnp.zeros((N_GROUPS, tm), jnp.int32)
    for o in range(N_GROUPS):
        ahead = (gscore[o] > gs) | ((gscore[o] == gs) & (o < gidx))
        beaten = beaten + ahead.astype(jnp.int32)
    keep = beaten < TOPK_GROUPS
    cand = jnp.concatenate([jnp.where(keep[g:g + 1, :], groups[g], -jnp.inf) for g in range(N_GROUPS)], axis=0)

    eidx = lax.broadcasted_iota(jnp.int32, (n_experts, tm), 0).astype(F32)
    chosen = jnp.zeros((n_experts, tm), F32)
    picks, tope_rows, score_rows = [], [], []
    for _ in range(TOP_K):
        idx = first_of(cand, cmax(cand), eidx, n_experts)
        hit = eidx == idx
        picks.append(hit)
        tope_rows.append(idx)
        score_rows.append(csum(jnp.where(hit, scores, 0.0)))
        chosen = jnp.where(hit, 1.0, chosen)
        cand = jnp.where(hit, -jnp.inf, cand)
    tope_ref[...] = jnp.concatenate(tope_rows, axis=0).astype(jnp.int32)
    s8 = jnp.concatenate(score_rows, axis=0)
    w8_ref[...] = s8 / csum(s8) * ROUTED_SCALE

    r = lax.broadcasted_iota(jnp.int32, (tm, tm), 0)
    c = lax.broadcasted_iota(jnp.int32, (tm, tm), 1)
    before = jnp.dot(chosen.astype(BF16), (r < c).astype(BF16), preferred_element_type=F32) + carry_ref[...]
    rank_rows = [csum(jnp.where(picks[k], before, 0.0)) for k in range(TOP_K)]
    rank8_ref[...] = jnp.concatenate(rank_rows, axis=0).astype(jnp.int32)
    carry_ref[...] += jnp.sum(chosen, axis=1, keepdims=True)
    counts_ref[...] = carry_ref[...].astype(jnp.int32)


def _router(x, w, bias):
    m, d = x.shape
    e = w.shape[1]
    tm = _tile(m, 256)
    const = lambda i: (0, 0)
    col = lambda i: (0, i)
    return pl.pallas_call(
        functools.partial(_router_kernel, tm=tm, n_experts=e),
        grid=(m // tm,),
        in_specs=[pl.BlockSpec((tm, d), lambda i: (i, 0)), pl.BlockSpec((e, d), const), pl.BlockSpec((e, 1), const)],
        out_specs=[pl.BlockSpec((TOP_K, tm), col), pl.BlockSpec((TOP_K, tm), col), pl.BlockSpec((TOP_K, tm), col),
                   pl.BlockSpec((e, 1), const)],
        out_shape=[jax.ShapeDtypeStruct((TOP_K, m), jnp.int32), jax.ShapeDtypeStruct((TOP_K, m), F32),
                   jax.ShapeDtypeStruct((TOP_K, m), jnp.int32), jax.ShapeDtypeStruct((e, 1), jnp.int32)],
        scratch_shapes=[pltpu.VMEM((e, 1), F32)],
        compiler_params=_params("arbitrary"),
        name="router",
    )(x, w.astype(F32).T, bias.astype(F32).reshape(e, 1))


def _dispatch_kernel(dest_ref, x_ref, init_ref, o_ref, sem, *, tm):
    del init_ref

    def send(r, carry):
        for k in range(TOP_K):
            d = dest_ref[0, 0, r * TOP_K + k]
            pltpu.make_async_copy(x_ref.at[pl.ds(r, 1), :], o_ref.at[pl.ds(d, 1), :], sem).start(priority=k % 2)
        return carry

    lax.fori_loop(0, tm, send, 0)
    for _ in range(TOP_K):
        pltpu.make_async_copy(x_ref, o_ref.at[pl.ds(0, tm), :], sem).wait()


def _dispatch(x, dest, init):
    n, d = x.shape
    tm = _tile(n, 256)
    assert init.shape[1] == d and init.dtype == x.dtype
    return pl.pallas_call(
        functools.partial(_dispatch_kernel, tm=tm),
        grid=(n // tm,),
        in_specs=[pl.BlockSpec((1, 1, tm * TOP_K), lambda i: (i, 0, 0), memory_space=pltpu.SMEM),
                  pl.BlockSpec((tm, d), lambda i: (i, 0)),
                  pl.BlockSpec(memory_space=pl.ANY)],
        out_specs=pl.BlockSpec(memory_space=pl.ANY),
        out_shape=jax.ShapeDtypeStruct(init.shape, x.dtype),
        scratch_shapes=[pltpu.SemaphoreType.DMA(())],
        input_output_aliases={2: 0},
        compiler_params=_params("arbitrary"),
        name="dispatch",
    )(dest.reshape(n // tm, 1, tm * TOP_K), x, init)


def _experts_kernel(be_ref, nxt_ref, slot_ref, nv_ref, x_ref, wgu_hbm, wdn_hbm, o_ref,
                    wgu_f32, wdn_f32, wgu_bf, wdn_bf, sems, *, d_expert, layer):
    i = pl.program_id(0)
    e = be_ref[i]
    slot = slot_ref[i]
    live = i < nv_ref[0]

    def weight_copies(expert, s):
        return (pltpu.make_async_copy(wgu_hbm.at[layer, expert], wgu_f32.at[s], sems.at[0, s]),
                pltpu.make_async_copy(wdn_hbm.at[layer, expert], wdn_f32.at[s], sems.at[1, s]))

    @pl.when(live & (i == 0))
    def _():
        for cp in weight_copies(e, slot):
            cp.start(priority=1)

    @pl.when(live & ((i == 0) | (e != be_ref[jnp.maximum(i - 1, 0)])))
    def _():
        for cp in weight_copies(e, slot):
            cp.wait()
        nxt = nxt_ref[i]

        @pl.when(nxt >= 0)
        def _():
            for cp in weight_copies(nxt, 1 - slot):
                cp.start(priority=1)

        wgu_bf[...] = wgu_f32[slot].astype(BF16)
        wdn_bf[...] = wdn_f32[slot].astype(BF16)

    @pl.when(live)
    def _():
        h = jnp.dot(_unpack_halves(x_ref[...]), wgu_bf[...], preferred_element_type=F32)
        act = jax.nn.silu(h[:, :d_expert]) * h[:, d_expert:]
        o_ref[...] = jnp.dot(act.astype(BF16), wdn_bf[...], preferred_element_type=F32).astype(o_ref.dtype)

    @pl.when(jnp.logical_not(live))
    def _():
        o_ref[...] = jnp.zeros_like(o_ref)


def _experts(x_rows, block_e, next_e, slot, n_valid, w_gu, w_down, layer):
    p, half = x_rows.shape
    d = w_gu.shape[2]
    assert d == 2 * half and x_rows.dtype == jnp.uint32
    d_expert = w_down.shape[2]
    tm = EXPERT_ROW_BLOCK
    grid_spec = pltpu.PrefetchScalarGridSpec(
        num_scalar_prefetch=4,
        grid=(p // tm,),
        in_specs=[pl.BlockSpec((tm, half), lambda i, be, nx, sl, nv: (jnp.maximum(jnp.minimum(i, nv[0] - 1), 0), 0)),
                  pl.BlockSpec(memory_space=pl.ANY),
                  pl.BlockSpec(memory_space=pl.ANY)],
        out_specs=pl.BlockSpec((tm, d), lambda i, be, nx, sl, nv: (i, 0)),
        scratch_shapes=[pltpu.VMEM((2, d, 2 * d_expert), F32), pltpu.VMEM((2, d_expert, d), F32),
                        pltpu.VMEM((d, 2 * d_expert), BF16), pltpu.VMEM((d_expert, d), BF16),
                        pltpu.SemaphoreType.DMA((2, 2))],
    )
    return pl.pallas_call(
        functools.partial(_experts_kernel, d_expert=d_expert, layer=layer),
        grid_spec=grid_spec,
        out_shape=jax.ShapeDtypeStruct((p, d), BF16),
        compiler_params=_params("arbitrary"),
        name="experts",
    )(block_e, next_e, slot, n_valid, x_rows, w_gu, w_down)


def _shared_ln_kernel(xb_ref, x_ref, yg_ref, w8_ref, wgu_ref, wdn_ref, g_ref, b_ref, of_ref, ob_ref,
                      *, alpha, d_expert):
    h = jnp.dot(xb_ref[...], wgu_ref[...], preferred_element_type=F32)
    act = jax.nn.silu(h[:, :d_expert]) * h[:, d_expert:]
    ffn = jnp.dot(act.astype(BF16), wdn_ref[...], preferred_element_type=F32)
    w8 = w8_ref[...]
    for k in range(TOP_K):
        ffn = ffn + w8[:, k:k + 1] * yg_ref[k].astype(F32)
    _layer_norm_store(alpha * x_ref[...] + ffn, g_ref, b_ref, of_ref, ob_ref)


def _shared_ln(xb, x, yg, w8, w_gu, w_down, g, b, alpha):
    m, d = x.shape
    d_expert = w_down.shape[0]
    tm = _tile(m, 256)
    row = lambda i: (i, 0)
    const = lambda i: (0, 0)
    return pl.pallas_call(
        functools.partial(_shared_ln_kernel, alpha=alpha, d_expert=d_expert),
        grid=(m // tm,),
        in_specs=[pl.BlockSpec((tm, d), row), pl.BlockSpec((tm, d), row),
                  pl.BlockSpec((TOP_K, tm, d), lambda i: (0, i, 0)),
                  pl.BlockSpec((tm, TOP_K), row),
                  pl.BlockSpec((d, 2 * d_expert), const), pl.BlockSpec((d_expert, d), const),
                  pl.BlockSpec((1, d), const), pl.BlockSpec((1, d), const)],
        out_specs=[pl.BlockSpec((tm, d), row), pl.BlockSpec((tm, d), row)],
        out_shape=[jax.ShapeDtypeStruct((m, d), F32), jax.ShapeDtypeStruct((m, d), BF16)],
        compiler_params=_params("parallel"),
        name="shared_ln",
    )(xb, x, yg, w8, w_gu, w_down, g.reshape(1, d), b.reshape(1, d))


def _moe(x, xb, xp, router_w, router_bias, w_gu, w_down, layer, sh_gu, sh_down, g, b, alpha, rows_buf):
    n, d = x.shape
    n_experts = router_w.shape[1]
    tm = EXPERT_ROW_BLOCK
    top_e, w8, rank8, counts = _router(x, router_w, router_bias)

    counts = counts[:, 0]
    experts = jnp.arange(n_experts, dtype=jnp.int32)
    padded = (counts + tm - 1) // tm * tm
    pend = jnp.cumsum(padded)
    pstart = pend - padded
    dest = rank8 + jnp.sum(jnp.where(top_e[..., None] == experts, pstart, 0), axis=-1)
    n_blocks = (n * TOP_K + n_experts * (tm - 1)) // tm + 1
    n_valid = (pend[-1] // tm).astype(jnp.int32)
    blk_start = jnp.minimum(jnp.arange(n_blocks, dtype=jnp.int32), n_valid - 1) * tm
    block_e = jnp.minimum(jnp.sum((pend[None, :] <= blk_start[:, None]).astype(jnp.int32), axis=1), n_experts - 1)
    nonempty = counts > 0
    later = (experts[None, :] > block_e[:, None]) & nonempty[None, :]
    next_e = jnp.min(jnp.where(later, experts[None, :], n_experts), axis=1)
    next_e = jnp.where(next_e == n_experts, -1, next_e).astype(jnp.int32)
    earlier = (experts[None, :] < block_e[:, None]) & nonempty[None, :]
    slot = (jnp.sum(earlier.astype(jnp.int32), axis=1) % 2).astype(jnp.int32)

    if rows_buf is None:
        rows_buf = jnp.zeros((n_blocks * tm, xp.shape[1]), xp.dtype)
    x_rows = _dispatch(xp, dest.T, rows_buf)
    y_rows = _experts(x_rows, block_e, next_e, slot, n_valid.reshape(1), w_gu, w_down, layer)
    yg = y_rows[dest.reshape(-1)].reshape(TOP_K, n, d)
    xf, xb = _shared_ln(xb, x, yg, w8.T, sh_gu, sh_down, g, b, alpha)
    return xf, xb, x_rows


def kernel(x, ln_mix_g, ln_mix_b, ln_ffn_g, ln_ffn_b, a_w_in, a_conv_w, a_conv_b, a_gate_w, a_gate_b, a_lambda, a_w_out, kv_w, b_w_q, b_w_o, router_w, router_bias, moe_w_gu, moe_w_down, sh_w_gu, sh_w_down):
    batch, seq, d = x.shape
    n = batch * seq
    n_a = a_w_in.shape[0]
    n_b = b_w_q.shape[0]
    depth = n_a + n_b
    alpha = (2.0 * depth) ** 0.25
    d_attn = b_w_q.shape[2]

    inv = ROPE_THETA ** (-jnp.arange(0, HEAD_DIM, 2, dtype=F32) / HEAD_DIM)
    ang = jnp.arange(seq, dtype=F32)[:, None] * inv[None, :]
    ang = jnp.concatenate([ang, ang], axis=-1)
    cos = jnp.cos(ang)
    sign = jnp.where(jnp.arange(HEAD_DIM) < HEAD_DIM // 2, -1.0, 1.0).astype(F32)
    sin_signed = jnp.sin(ang) * sign
    q_scale = HEAD_DIM ** -0.5 * math.log2(math.e)

    xf = x.reshape(n, d)
    xb = xf.astype(BF16)
    kv = k_mean = v_t = rows_buf = None
    for layer in range(depth):
        if layer < n_a:
            proj = _proj(xb, a_w_in[layer].astype(BF16), F32)
            y = _rglru(proj, a_conv_w[layer], a_conv_b[layer], a_gate_w[layer].astype(BF16), a_gate_b[layer],
                       jax.nn.softplus(-a_lambda[layer].astype(F32)), batch, seq)
            w_o = a_w_out[layer]
        else:
            j = layer - n_a
            if j == 0:
                kv, k_mean = _rope_proj(xb, kv_w.astype(BF16), cos, sin_signed, d_attn, seq)
                k_mean = k_mean.reshape(batch, seq // MOBA_BLOCK, 2 * d_attn)
                v_t = jnp.swapaxes(kv[:, d_attn:].reshape(batch, seq // MOBA_BLOCK, MOBA_BLOCK, d_attn), 2, 3)
            q, _ = _rope_proj(xb, b_w_q[j].astype(BF16), cos * q_scale, sin_signed * q_scale, d_attn, seq)
            y = _moba(q, kv, v_t, k_mean, batch, seq)
            w_o = b_w_o[j]
        xf, xb, xp = _mm_res_ln(y, w_o.astype(BF16), xf, ln_mix_g[layer], ln_mix_b[layer], alpha)
        xf, xb, rows_buf = _moe(xf, xb, xp, router_w[layer], router_bias[layer], moe_w_gu, moe_w_down, layer,
                                sh_w_gu[layer].astype(BF16), sh_w_down[layer].astype(BF16),
                                ln_ffn_g[layer], ln_ffn_b[layer], alpha, rows_buf)
    return xf.reshape(batch, seq, d)
```

```python
import functools
import math

import jax
import jax.numpy as jnp
from jax import lax
from jax.experimental import pallas as pl
from jax.experimental.pallas import tpu as pltpu

HEAD_DIM = 128
MOBA_BLOCK = 256
MOBA_TOPK = 3
ROPE_THETA = 10000.0
CONV_WIDTH = 4
LRU_C = 8.0
TOP_K = 8
N_GROUPS = 8
TOPK_GROUPS = 4
ROUTED_SCALE = 2.5
LN_EPS = 1e-5
NEG = -1e30

VMEM_LIMIT_BYTES = 56 * 1024 * 1024
SUBLANES = 8
EXPERT_ROW_BLOCK = 256

F32 = jnp.float32
BF16 = jnp.bfloat16


def _params(*sem):
    return pltpu.CompilerParams(dimension_semantics=sem, vmem_limit_bytes=VMEM_LIMIT_BYTES)


def _tile(n, pref):
    t = min(n, pref)
    assert n % t == 0, (n, t)
    return t


def _proj_kernel(x_ref, w_ref, o_ref):
    o_ref[...] = jnp.dot(x_ref[...], w_ref[...], preferred_element_type=F32).astype(o_ref.dtype)


def _proj(x, w, out_dtype):
    m, k = x.shape
    n = w.shape[1]
    tm, tn = _tile(m, 1024), _tile(n, 1024)
    return pl.pallas_call(
        _proj_kernel,
        grid=(m // tm, n // tn),
        in_specs=[pl.BlockSpec((tm, k), lambda i, j: (i, 0)),
                  pl.BlockSpec((k, tn), lambda i, j: (0, j))],
        out_specs=pl.BlockSpec((tm, tn), lambda i, j: (i, j)),
        out_shape=jax.ShapeDtypeStruct((m, n), out_dtype),
        compiler_params=_params("parallel", "arbitrary"),
        name="proj",
    )(x, w)


def _rope_proj_kernel(x_ref, w_ref, cos_ref, sin_ref, o_ref, km_ref, *, n_rope_tiles, tm, tn):
    acc = jnp.dot(x_ref[...], w_ref[...], preferred_element_type=F32)
    j = pl.program_id(1)
    nb = tm // MOBA_BLOCK

    def block_means(val, sl):
        for r in range(nb):
            km_ref[r, :, sl] = jnp.mean(val[r * MOBA_BLOCK:(r + 1) * MOBA_BLOCK], axis=0, keepdims=True)

    @pl.when(j < n_rope_tiles)
    def _():
        cos = cos_ref[...]
        sin = sin_ref[...]
        for h in range(tn // HEAD_DIM):
            sl = slice(h * HEAD_DIM, (h + 1) * HEAD_DIM)
            seg = acc[:, sl]
            roped = seg * cos + pltpu.roll(seg, HEAD_DIM // 2, axis=1) * sin
            o_ref[:, sl] = roped.astype(o_ref.dtype)
            block_means(roped, sl)

    @pl.when(j >= n_rope_tiles)
    def _():
        o_ref[...] = acc.astype(o_ref.dtype)
        block_means(acc, slice(None))


def _rope_proj(x, w, cos, sin_signed, n_rope, seq):
    m, k = x.shape
    n = w.shape[1]
    tm = _tile(seq, 1024)
    tn = _tile(n_rope, 1024)
    assert n % tn == 0 and tm % MOBA_BLOCK == 0
    n_seq_tiles = seq // tm
    kern = functools.partial(_rope_proj_kernel, n_rope_tiles=n_rope // tn, tm=tm, tn=tn)
    return pl.pallas_call(
        kern,
        grid=(m // tm, n // tn),
        in_specs=[pl.BlockSpec((tm, k), lambda i, j: (i, 0)),
                  pl.BlockSpec((k, tn), lambda i, j: (0, j)),
                  pl.BlockSpec((tm, HEAD_DIM), lambda i, j: (i % n_seq_tiles, 0)),
                  pl.BlockSpec((tm, HEAD_DIM), lambda i, j: (i % n_seq_tiles, 0))],
        out_specs=[pl.BlockSpec((tm, tn), lambda i, j: (i, j)),
                   pl.BlockSpec((tm // MOBA_BLOCK, 1, tn), lambda i, j: (i, 0, j))],
        out_shape=[jax.ShapeDtypeStruct((m, n), BF16),
                   jax.ShapeDtypeStruct((m // MOBA_BLOCK, 1, n), F32)],
        compiler_params=_params("parallel", "arbitrary"),
        name="rope_proj",
    )(x, w, cos, sin_signed)


def _bf16_bits(v):
    return lax.bitcast_convert_type(v.astype(BF16).astype(F32), jnp.uint32)


def _pack_halves(v):
    half = v.shape[1] // 2
    return (_bf16_bits(v[:, :half]) >> 16) | _bf16_bits(v[:, half:])


def _unpack_halves(u):
    lo = lax.bitcast_convert_type(u << 16, F32).astype(BF16)
    hi = lax.bitcast_convert_type(u & jnp.uint32(0xFFFF0000), F32).astype(BF16)
    return jnp.concatenate([lo, hi], axis=1)


def _layer_norm_store(z, g_ref, b_ref, of_ref, ob_ref, op_ref=None):
    mu = jnp.mean(z, axis=-1, keepdims=True)
    zc = z - mu
    var = jnp.mean(zc * zc, axis=-1, keepdims=True)
    out = zc * lax.rsqrt(var + LN_EPS) * g_ref[...] + b_ref[...]
    of_ref[...] = out
    ob_ref[...] = out.astype(BF16)
    if op_ref is not None:
        op_ref[...] = _pack_halves(out)


def _mm_res_ln_kernel(y_ref, w_ref, x_ref, g_ref, b_ref, of_ref, ob_ref, op_ref, *, alpha):
    mix = jnp.dot(y_ref[...], w_ref[...], preferred_element_type=F32)
    _layer_norm_store(alpha * x_ref[...] + mix, g_ref, b_ref, of_ref, ob_ref, op_ref)


def _mm_res_ln(y, w, x, g, b, alpha):
    m, k = y.shape
    d = w.shape[1]
    tm = _tile(m, 256)
    row = lambda i: (i, 0)
    const = lambda i: (0, 0)
    return pl.pallas_call(
        functools.partial(_mm_res_ln_kernel, alpha=alpha),
        grid=(m // tm,),
        in_specs=[pl.BlockSpec((tm, k), row), pl.BlockSpec((k, d), const), pl.BlockSpec((tm, d), row),
                  pl.BlockSpec((1, d), const), pl.BlockSpec((1, d), const)],
        out_specs=[pl.BlockSpec((tm, d), row), pl.BlockSpec((tm, d), row), pl.BlockSpec((tm, d // 2), row)],
        out_shape=[jax.ShapeDtypeStruct((m, d), F32), jax.ShapeDtypeStruct((m, d), BF16),
                   jax.ShapeDtypeStruct((m, d // 2), jnp.uint32)],
        compiler_params=_params("parallel"),
        name="mm_res_ln",
    )(y, w, x, g.reshape(1, d), b.reshape(1, d))


def _rglru_kernel(gb_ref, u_ref, cw_ref, cb_ref, gw_ref, gbias_ref, sp_ref, y_ref,
                  uext_ref, h_ref, a_ref, b_ref, *, ts, n_blocks, cblk):
    t = pl.program_id(1)

    @pl.when(t == 0)
    def _():
        uext_ref[0:SUBLANES, :] = jnp.zeros((SUBLANES, uext_ref.shape[1]), F32)
        h_ref[...] = jnp.zeros_like(h_ref)

    uext_ref[SUBLANES:SUBLANES + ts, :] = u_ref[...]
    u = cb_ref[...] + sum(
        uext_ref[pl.ds(SUBLANES - (CONV_WIDTH - 1) + k, ts), :] * cw_ref[k:k + 1, :]
        for k in range(CONV_WIDTH))
    uext_ref[0:SUBLANES, :] = uext_ref[ts:ts + SUBLANES, :]

    for n in range(n_blocks):
        sl = slice(n * cblk, (n + 1) * cblk)
        un = u[:, sl]
        gates = jnp.dot(un.astype(BF16), gw_ref[n], preferred_element_type=F32) + gbias_ref[n]
        gates = jax.nn.sigmoid(gates)
        r = gates[:, :cblk]
        i_gate = gates[:, cblk:]
        log_a = -LRU_C * r * sp_ref[:, sl]
        a = jnp.exp(log_a)
        mult = jnp.sqrt(1.0 - jnp.exp(2.0 * log_a))
        a_ref[:, sl] = a
        b_ref[:, sl] = mult * i_gate * un

    width = a_ref.shape[1]
    row = lax.broadcasted_iota(jnp.int32, (SUBLANES, width), 0)

    def slab(s, h):
        off = pl.multiple_of(s * SUBLANES, SUBLANES)
        a = a_ref[pl.ds(off, SUBLANES), :]
        b = b_ref[pl.ds(off, SUBLANES), :]
        for sh in (1, 2, 4):
            keep = row >= sh
            a_prev = jnp.where(keep, pltpu.roll(a, sh, axis=0), 1.0)
            b_prev = jnp.where(keep, pltpu.roll(b, sh, axis=0), 0.0)
            b = a * b_prev + b
            a = a * a_prev
        hs = a * h + b
        b_ref[pl.ds(off, SUBLANES), :] = hs
        return hs[SUBLANES - 1:SUBLANES, :]

    h_ref[...] = lax.fori_loop(0, ts // SUBLANES, slab, h_ref[...])
    y_ref[...] = (jax.nn.gelu(gb_ref[...], approximate=True) * b_ref[...]).astype(y_ref.dtype)


def _rglru(proj, conv_w, conv_b, gate_w, gate_b, softplus_neg_lam, batch, seq):
    m, two_d = proj.shape
    d = two_d // 2
    n_blocks, cblk = gate_w.shape[0], gate_w.shape[1]
    ts = _tile(seq, 256)
    nt = seq // ts
    kern = functools.partial(_rglru_kernel, ts=ts, n_blocks=n_blocks, cblk=cblk)
    const2 = lambda b, t: (0, 0)
    const3 = lambda b, t: (0, 0, 0)
    return pl.pallas_call(
        kern,
        grid=(batch, nt),
        in_specs=[pl.BlockSpec((ts, d), lambda b, t: (b * nt + t, 0)),
                  pl.BlockSpec((ts, d), lambda b, t: (b * nt + t, 1)),
                  pl.BlockSpec((CONV_WIDTH, d), const2),
                  pl.BlockSpec((1, d), const2),
                  pl.BlockSpec((n_blocks, cblk, 2 * cblk), const3),
                  pl.BlockSpec((n_blocks, 1, 2 * cblk), const3),
                  pl.BlockSpec((1, d), const2)],
        out_specs=pl.BlockSpec((ts, d), lambda b, t: (b * nt + t, 0)),
        out_shape=jax.ShapeDtypeStruct((m, d), BF16),
        scratch_shapes=[pltpu.VMEM((ts + SUBLANES, d), F32), pltpu.VMEM((1, d), F32),
                        pltpu.VMEM((ts, d), F32), pltpu.VMEM((ts, d), F32)],
        compiler_params=_params("arbitrary", "arbitrary"),
        name="rglru",
    )(proj, proj, conv_w, conv_b.reshape(1, d), gate_w, gate_b.reshape(n_blocks, 1, 2 * cblk),
      softplus_neg_lam.reshape(1, d))


def _moba_kernel(q_ref, k_ref, vt_ref, km_ref, o_ref, acc_ref, *, heads, n_kv_blocks):
    i = pl.program_id(2)
    L = MOBA_BLOCK
    nt_dims = (((1,), (1,)), ((), ()))
    blk = lax.broadcasted_iota(jnp.int32, (n_kv_blocks, L), 0)
    key = lax.broadcasted_iota(jnp.int32, (L, L), 0)
    qry = lax.broadcasted_iota(jnp.int32, (L, L), 1)
    sls = [slice(h * HEAD_DIM, (h + 1) * HEAD_DIM) for h in range(heads)]
    qs = [q_ref[:, sl] for sl in sls]

    bits = []
    for h in range(heads):
        gate = lax.dot_general(km_ref[0, :, sls[h]], qs[h].astype(F32), nt_dims,
                               precision=lax.Precision.HIGHEST, preferred_element_type=F32)
        gate = jnp.where(blk < i, gate, -jnp.inf)
        chosen = jnp.zeros((n_kv_blocks, L), F32)
        for _ in range(MOBA_TOPK):
            best = jnp.max(gate, axis=0, keepdims=True)
            first = jnp.min(jnp.where(gate == best, blk, n_kv_blocks), axis=0, keepdims=True)
            pick = (blk == first) & (best > -jnp.inf)
            chosen = jnp.where(pick, 1.0, chosen)
            gate = jnp.where(pick, -jnp.inf, gate)
        weight = jnp.left_shift(1, blk).astype(F32)
        bits.append(jnp.sum(chosen * weight, axis=0, keepdims=True).astype(jnp.int32))
        acc_ref[h] = jnp.zeros((HEAD_DIM, L), F32)

    def scores(koff):
        return [lax.dot_general(k_ref[pl.ds(koff, L), sls[h]], qs[h], nt_dims, preferred_element_type=F32)
                for h in range(heads)]

    def accumulate(jblk, ps):
        for h in range(heads):
            acc_ref[h] += jnp.dot(vt_ref[0, jblk, sls[h], :], ps[h], preferred_element_type=F32)

    def past(j, carry):
        ss = scores(pl.multiple_of(j * L, L))
        ps, out = [], []
        for h in range(heads):
            m, l = carry[h]
            picked = (jnp.right_shift(bits[h], j) & 1) > 0
            m_new = jnp.where(picked, jnp.maximum(m, jnp.max(ss[h], axis=0, keepdims=True)), m)
            alpha = jnp.exp2(m - m_new)
            p = jnp.exp2(ss[h] - jnp.where(picked, m_new, -NEG))
            l = alpha * l + jnp.sum(p, axis=0, keepdims=True)
            acc_ref[h] = alpha * acc_ref[h]
            ps.append(p.astype(BF16))
            out.append((m_new, l))
        accumulate(j, ps)
        return tuple(out)

    def own(carry):
        ss = scores(pl.multiple_of(i * L, L))
        ps, ls = [], []
        for h in range(heads):
            m, l = carry[h]
            s = jnp.where(key <= qry, ss[h], NEG)
            m_new = jnp.maximum(m, jnp.max(s, axis=0, keepdims=True))
            alpha = jnp.exp2(m - m_new)
            p = jnp.exp2(s - m_new)
            ls.append(alpha * l + jnp.sum(p, axis=0, keepdims=True))
            acc_ref[h] = alpha * acc_ref[h]
            ps.append(p.astype(BF16))
        accumulate(i, ps)
        return ls

    init = tuple((jnp.full((1, L), NEG, F32), jnp.zeros((1, L), F32)) for _ in range(heads))
    ls = own(lax.fori_loop(0, i, past, init))
    for h in range(heads):
        o_ref[:, sls[h]] = (acc_ref[h] / ls[h]).T.astype(o_ref.dtype)


def _moba(q, kv, vt, k_mean, batch, seq):
    m, d_attn = q.shape
    heads = min(16, d_attn // HEAD_DIM)
    gw = heads * HEAD_DIM
    n_groups = d_attn // gw
    nq = seq // MOBA_BLOCK
    kern = functools.partial(_moba_kernel, heads=heads, n_kv_blocks=nq)
    return pl.pallas_call(
        kern,
        grid=(batch, n_groups, nq),
        in_specs=[pl.BlockSpec((MOBA_BLOCK, gw), lambda b, g, i: (b * nq + i, g)),
                  pl.BlockSpec((seq, gw), lambda b, g, i: (b, g), pipeline_mode=pl.Buffered(1)),
                  pl.BlockSpec((1, nq, gw, MOBA_BLOCK), lambda b, g, i: (b, 0, g, 0), pipeline_mode=pl.Buffered(1)),
                  pl.BlockSpec((1, nq, gw), lambda b, g, i: (b, 0, g))],
        out_specs=pl.BlockSpec((MOBA_BLOCK, gw), lambda b, g, i: (b * nq + i, g)),
        out_shape=jax.ShapeDtypeStruct((m, d_attn), BF16),
        scratch_shapes=[pltpu.VMEM((heads, HEAD_DIM, MOBA_BLOCK), F32)],
        compiler_params=_params("parallel", "parallel", "arbitrary"),
        name="moba",
    )(q, kv, vt, k_mean)


def _router_kernel(x_ref, wt_ref, bias_ref, tope_ref, w8_ref, rank8_ref, counts_ref, carry_ref,
                   *, tm, n_experts):
    @pl.when(pl.program_id(0) == 0)
    def _():
        carry_ref[...] = jnp.zeros_like(carry_ref)

    nt_dims = (((1,), (1,)), ((), ()))
    hi_lo = lambda v: (v.astype(BF16), (v - v.astype(BF16).astype(F32)).astype(BF16))
    xh, xl = hi_lo(x_ref[...])
    wh, wl = hi_lo(wt_ref[...])
    ntdot = lambda a, b: lax.dot_general(a, b, nt_dims, preferred_element_type=F32)
    logits = ntdot(wh, xh) + (ntdot(wl, xh) + ntdot(wh, xl))
    scores = jax.nn.sigmoid(logits)
    sel = scores + bias_ref[...]

    gsz = n_experts // N_GROUPS
    cmax = lambda v: jnp.max(v, axis=0, keepdims=True)
    csum = lambda v: jnp.sum(v, axis=0, keepdims=True)
    first_of = lambda v, m, idx, n: jnp.min(jnp.where(v == m, idx, float(n)), axis=0, keepdims=True)

    in_group = lax.broadcasted_iota(jnp.int32, (gsz, tm), 0).astype(F32)
    groups = [sel[g * gsz:(g + 1) * gsz, :] for g in range(N_GROUPS)]
    gscore = []
    for sg in groups:
        m1 = cmax(sg)
        m2 = cmax(jnp.where(in_group == first_of(sg, m1, in_group, gsz), -jnp.inf, sg))
        gscore.append(m1 + m2)
    gidx = lax.broadcasted_iota(jnp.int32, (N_GROUPS, tm), 0)
    gs = jnp.concatenate(gscore, axis=0)
    beaten = jnp.zeros((N_GROUPS, tm), jnp.int32)
    for o in range(N_GROUPS):
        ahead = (gscore[o] > gs) | ((gscore[o] == gs) & (o < gidx))
        beaten = beaten + ahead.astype(jnp.int32)
    keep = beaten < TOPK_GROUPS
    cand = jnp.concatenate([jnp.where(keep[g:g + 1, :], groups[g], -jnp.inf) for g in range(N_GROUPS)], axis=0)

    eidx = lax.broadcasted_iota(jnp.int32, (n_experts, tm), 0).astype(F32)
    chosen = jnp.zeros((n_experts, tm), F32)
    picks, tope_rows, score_rows = [], [], []
    for _ in range(TOP_K):
        idx = first_of(cand, cmax(cand), eidx, n_experts)
        hit = eidx == idx
        picks.append(hit)
        tope_rows.append(idx)
        score_rows.append(csum(jnp.where(hit, scores, 0.0)))
        chosen = jnp.where(hit, 1.0, chosen)
        cand = jnp.where(hit, -jnp.inf, cand)
    tope_ref[...] = jnp.concatenate(tope_rows, axis=0).astype(jnp.int32)
    s8 = jnp.concatenate(score_rows, axis=0)
    w8_ref[...] = s8 / csum(s8) * ROUTED_SCALE

    r = lax.broadcasted_iota(jnp.int32, (tm, tm), 0)
    c = lax.broadcasted_iota(jnp.int32, (tm, tm), 1)
    before = jnp.dot(chosen.astype(BF16), (r < c).astype(BF16), preferred_element_type=F32) + carry_ref[...]
    rank_rows = [csum(jnp.where(picks[k], before, 0.0)) for k in range(TOP_K)]
    rank8_ref[...] = jnp.concatenate(rank_rows, axis=0).astype(jnp.int32)
    carry_ref[...] += jnp.sum(chosen, axis=1, keepdims=True)
    counts_ref[...] = carry_ref[...].astype(jnp.int32)


def _router(x, w, bias):
    m, d = x.shape
    e = w.shape[1]
    tm = _tile(m, 256)
    const = lambda i: (0, 0)
    col = lambda i: (0, i)
    return pl.pallas_call(
        functools.partial(_router_kernel, tm=tm, n_experts=e),
        grid=(m // tm,),
        in_specs=[pl.BlockSpec((tm, d), lambda i: (i, 0)), pl.BlockSpec((e, d), const), pl.BlockSpec((e, 1), const)],
        out_specs=[pl.BlockSpec((TOP_K, tm), col), pl.BlockSpec((TOP_K, tm), col), pl.BlockSpec((TOP_K, tm), col),
                   pl.BlockSpec((e, 1), const)],
        out_shape=[jax.ShapeDtypeStruct((TOP_K, m), jnp.int32), jax.ShapeDtypeStruct((TOP_K, m), F32),
                   jax.ShapeDtypeStruct((TOP_K, m), jnp.int32), jax.ShapeDtypeStruct((e, 1), jnp.int32)],
        scratch_shapes=[pltpu.VMEM((e, 1), F32)],
        compiler_params=_params("arbitrary"),
        name="router",
    )(x, w.astype(F32).T, bias.astype(F32).reshape(e, 1))


def _dispatch_kernel(dest_ref, x_ref, init_ref, o_ref, sem, *, tm):
    del init_ref

    def send(r, carry):
        for k in range(TOP_K):
            d = dest_ref[0, 0, r * TOP_K + k]
            pltpu.make_async_copy(x_ref.at[pl.ds(r, 1), :], o_ref.at[pl.ds(d, 1), :], sem).start(priority=k % 2)
        return carry

    lax.fori_loop(0, tm, send, 0)
    for _ in range(TOP_K):
        pltpu.make_async_copy(x_ref, o_ref.at[pl.ds(0, tm), :], sem).wait()


def _dispatch(x, dest, init):
    n, d = x.shape
    tm = _tile(n, 256)
    assert init.shape[1] == d and init.dtype == x.dtype
    return pl.pallas_call(
        functools.partial(_dispatch_kernel, tm=tm),
        grid=(n // tm,),
        in_specs=[pl.BlockSpec((1, 1, tm * TOP_K), lambda i: (i, 0, 0), memory_space=pltpu.SMEM),
                  pl.BlockSpec((tm, d), lambda i: (i, 0)),
                  pl.BlockSpec(memory_space=pl.ANY)],
        out_specs=pl.BlockSpec(memory_space=pl.ANY),
        out_shape=jax.ShapeDtypeStruct(init.shape, x.dtype),
        scratch_shapes=[pltpu.SemaphoreType.DMA(())],
        input_output_aliases={2: 0},
        compiler_params=_params("arbitrary"),
        name="dispatch",
    )(dest.reshape(n // tm, 1, tm * TOP_K), x, init)


def _experts_kernel(be_ref, nxt_ref, slot_ref, nv_ref, x_ref, wgu_hbm, wdn_hbm, o_ref,
                    wgu_f32, wdn_f32, wgu_bf, wdn_bf, sems, *, d_expert, layer):
    i = pl.program_id(0)
    e = be_ref[i]
    slot = slot_ref[i]
    live = i < nv_ref[0]

    def weight_copies(expert, s):
        return (pltpu.make_async_copy(wgu_hbm.at[layer, expert], wgu_f32.at[s], sems.at[0, s]),
                pltpu.make_async_copy(wdn_hbm.at[layer, expert], wdn_f32.at[s], sems.at[1, s]))

    @pl.when(live & (i == 0))
    def _():
        for cp in weight_copies(e, slot):
            cp.start(priority=1)

    @pl.when(live & ((i == 0) | (e != be_ref[jnp.maximum(i - 1, 0)])))
    def _():
        for cp in weight_copies(e, slot):
            cp.wait()
        nxt = nxt_ref[i]

        @pl.when(nxt >= 0)
        def _():
            for cp in weight_copies(nxt, 1 - slot):
                cp.start(priority=1)

        wgu_bf[...] = wgu_f32[slot].astype(BF16)
        wdn_bf[...] = wdn_f32[slot].astype(BF16)

    @pl.when(live)
    def _():
        h = jnp.dot(_unpack_halves(x_ref[...]), wgu_bf[...], preferred_element_type=F32)
        act = jax.nn.silu(h[:, :d_expert]) * h[:, d_expert:]
        o_ref[...] = _pack_halves(jnp.dot(act.astype(BF16), wdn_bf[...], preferred_element_type=F32))

    @pl.when(jnp.logical_not(live))
    def _():
        o_ref[...] = jnp.zeros_like(o_ref)


def _experts(x_rows, block_e, next_e, slot, n_valid, w_gu, w_down, layer):
    p, half = x_rows.shape
    d = w_gu.shape[2]
    assert d == 2 * half and x_rows.dtype == jnp.uint32
    d_expert = w_down.shape[2]
    tm = EXPERT_ROW_BLOCK
    grid_spec = pltpu.PrefetchScalarGridSpec(
        num_scalar_prefetch=4,
        grid=(p // tm,),
        in_specs=[pl.BlockSpec((tm, half), lambda i, be, nx, sl, nv: (jnp.maximum(jnp.minimum(i, nv[0] - 1), 0), 0)),
                  pl.BlockSpec(memory_space=pl.ANY),
                  pl.BlockSpec(memory_space=pl.ANY)],
        out_specs=pl.BlockSpec((tm, half), lambda i, be, nx, sl, nv: (i, 0)),
        scratch_shapes=[pltpu.VMEM((2, d, 2 * d_expert), F32), pltpu.VMEM((2, d_expert, d), F32),
                        pltpu.VMEM((d, 2 * d_expert), BF16), pltpu.VMEM((d_expert, d), BF16),
                        pltpu.SemaphoreType.DMA((2, 2))],
    )
    return pl.pallas_call(
        functools.partial(_experts_kernel, d_expert=d_expert, layer=layer),
        grid_spec=grid_spec,
        out_shape=jax.ShapeDtypeStruct((p, half), jnp.uint32),
        compiler_params=_params("arbitrary"),
        name="experts",
    )(block_e, next_e, slot, n_valid, x_rows, w_gu, w_down)


def _unpack_halves_f32(u):
    lo = lax.bitcast_convert_type(u << 16, F32)
    hi = lax.bitcast_convert_type(u & jnp.uint32(0xFFFF0000), F32)
    return jnp.concatenate([lo, hi], axis=1)


def _shared_ln_kernel(dest_ref, xb_ref, x_ref, y_hbm, w8_ref, wgu_ref, wdn_ref, g_ref, b_ref, of_ref, ob_ref,
                      ybuf, sem, *, alpha, d_expert, tm):
    def fetch(r, carry):
        for k in range(TOP_K):
            d = dest_ref[0, 0, r * TOP_K + k]
            pltpu.make_async_copy(y_hbm.at[pl.ds(d, 1), :], ybuf.at[k, pl.ds(r, 1), :], sem).start(priority=k % 2)
        return carry

    lax.fori_loop(0, tm, fetch, 0)
    h = jnp.dot(xb_ref[...], wgu_ref[...], preferred_element_type=F32)
    act = jax.nn.silu(h[:, :d_expert]) * h[:, d_expert:]
    ffn = jnp.dot(act.astype(BF16), wdn_ref[...], preferred_element_type=F32)
    for k in range(TOP_K):
        pltpu.make_async_copy(y_hbm.at[pl.ds(0, tm), :], ybuf.at[k], sem).wait()
    w8 = w8_ref[...]
    for k in range(TOP_K):
        ffn = ffn + w8[:, k:k + 1] * _unpack_halves_f32(ybuf[k])
    _layer_norm_store(alpha * x_ref[...] + ffn, g_ref, b_ref, of_ref, ob_ref)


def _shared_ln(xb, x, y_rows, dest, w8, w_gu, w_down, g, b, alpha):
    m, d = x.shape
    d_expert = w_down.shape[0]
    tm = _tile(m, 256)
    assert y_rows.shape[1] * 2 == d and y_rows.dtype == jnp.uint32
    row = lambda i: (i, 0)
    const = lambda i: (0, 0)
    return pl.pallas_call(
        functools.partial(_shared_ln_kernel, alpha=alpha, d_expert=d_expert, tm=tm),
        grid=(m // tm,),
        in_specs=[pl.BlockSpec((1, 1, tm * TOP_K), lambda i: (i, 0, 0), memory_space=pltpu.SMEM),
                  pl.BlockSpec((tm, d), row), pl.BlockSpec((tm, d), row),
                  pl.BlockSpec(memory_space=pl.ANY),
                  pl.BlockSpec((tm, TOP_K), row),
                  pl.BlockSpec((d, 2 * d_expert), const), pl.BlockSpec((d_expert, d), const),
                  pl.BlockSpec((1, d), const), pl.BlockSpec((1, d), const)],
        out_specs=[pl.BlockSpec((tm, d), row), pl.BlockSpec((tm, d), row)],
        out_shape=[jax.ShapeDtypeStruct((m, d), F32), jax.ShapeDtypeStruct((m, d), BF16)],
        scratch_shapes=[pltpu.VMEM((TOP_K, tm, d // 2), jnp.uint32), pltpu.SemaphoreType.DMA(())],
        compiler_params=_params("arbitrary"),
        name="shared_ln",
    )(dest.reshape(m // tm, 1, tm * TOP_K), xb, x, y_rows, w8, w_gu, w_down, g.reshape(1, d), b.reshape(1, d))


def _moe(x, xb, xp, router_w, router_bias, w_gu, w_down, layer, sh_gu, sh_down, g, b, alpha, rows_buf):
    n, d = x.shape
    n_experts = router_w.shape[1]
    tm = EXPERT_ROW_BLOCK
    top_e, w8, rank8, counts = _router(x, router_w, router_bias)

    counts = counts[:, 0]
    experts = jnp.arange(n_experts, dtype=jnp.int32)
    padded = (counts + tm - 1) // tm * tm
    pend = jnp.cumsum(padded)
    pstart = pend - padded
    dest = rank8 + jnp.sum(jnp.where(top_e[..., None] == experts, pstart, 0), axis=-1)
    n_blocks = (n * TOP_K + n_experts * (tm - 1)) // tm + 1
    n_valid = (pend[-1] // tm).astype(jnp.int32)
    blk_start = jnp.minimum(jnp.arange(n_blocks, dtype=jnp.int32), n_valid - 1) * tm
    block_e = jnp.minimum(jnp.sum((pend[None, :] <= blk_start[:, None]).astype(jnp.int32), axis=1), n_experts - 1)
    nonempty = counts > 0
    later = (experts[None, :] > block_e[:, None]) & nonempty[None, :]
    next_e = jnp.min(jnp.where(later, experts[None, :], n_experts), axis=1)
    next_e = jnp.where(next_e == n_experts, -1, next_e).astype(jnp.int32)
    earlier = (experts[None, :] < block_e[:, None]) & nonempty[None, :]
    slot = (jnp.sum(earlier.astype(jnp.int32), axis=1) % 2).astype(jnp.int32)

    if rows_buf is None:
        rows_buf = jnp.zeros((n_blocks * tm, xp.shape[1]), xp.dtype)
    dest_tok = dest.T
    x_rows = _dispatch(xp, dest_tok, rows_buf)
    y_rows = _experts(x_rows, block_e, next_e, slot, n_valid.reshape(1), w_gu, w_down, layer)
    xf, xb = _shared_ln(xb, x, y_rows, dest_tok, w8.T, sh_gu, sh_down, g, b, alpha)
    return xf, xb, x_rows


def kernel(x, ln_mix_g, ln_mix_b, ln_ffn_g, ln_ffn_b, a_w_in, a_conv_w, a_conv_b, a_gate_w, a_gate_b, a_lambda, a_w_out, kv_w, b_w_q, b_w_o, router_w, router_bias, moe_w_gu, moe_w_down, sh_w_gu, sh_w_down):
    batch, seq, d = x.shape
    n = batch * seq
    n_a = a_w_in.shape[0]
    n_b = b_w_q.shape[0]
    depth = n_a + n_b
    alpha = (2.0 * depth) ** 0.25
    d_attn = b_w_q.shape[2]

    inv = ROPE_THETA ** (-jnp.arange(0, HEAD_DIM, 2, dtype=F32) / HEAD_DIM)
    ang = jnp.arange(seq, dtype=F32)[:, None] * inv[None, :]
    ang = jnp.concatenate([ang, ang], axis=-1)
    cos = jnp.cos(ang)
    sign = jnp.where(jnp.arange(HEAD_DIM) < HEAD_DIM // 2, -1.0, 1.0).astype(F32)
    sin_signed = jnp.sin(ang) * sign
    q_scale = HEAD_DIM ** -0.5 * math.log2(math.e)

    xf = x.reshape(n, d)
    xb = xf.astype(BF16)
    kv = k_mean = v_t = rows_buf = None
    for layer in range(depth):
        if layer < n_a:
            proj = _proj(xb, a_w_in[layer].astype(BF16), F32)
            y = _rglru(proj, a_conv_w[layer], a_conv_b[layer], a_gate_w[layer].astype(BF16), a_gate_b[layer],
                       jax.nn.softplus(-a_lambda[layer].astype(F32)), batch, seq)
            w_o = a_w_out[layer]
        else:
            j = layer - n_a
            if j == 0:
                kv, k_mean = _rope_proj(xb, kv_w.astype(BF16), cos, sin_signed, d_attn, seq)
                k_mean = k_mean.reshape(batch, seq // MOBA_BLOCK, 2 * d_attn)
                v_t = jnp.swapaxes(kv[:, d_attn:].reshape(batch, seq // MOBA_BLOCK, MOBA_BLOCK, d_attn), 2, 3)
            q, _ = _rope_proj(xb, b_w_q[j].astype(BF16), cos * q_scale, sin_signed * q_scale, d_attn, seq)
            y = _moba(q, kv, v_t, k_mean, batch, seq)
            w_o = b_w_o[j]
        xf, xb, xp = _mm_res_ln(y, w_o.astype(BF16), xf, ln_mix_g[layer], ln_mix_b[layer], alpha)
        xf, xb, rows_buf = _moe(xf, xb, xp, router_w[layer], router_bias[layer], moe_w_gu, moe_w_down, layer,
                                sh_w_gu[layer].astype(BF16), sh_w_down[layer].astype(BF16),
                                ln_ffn_g[layer], ln_ffn_b[layer], alpha, rows_buf)
    return xf.reshape(batch, seq, d)
```

```python
import functools
import math

import jax
import jax.numpy as jnp
from jax import lax
from jax.experimental import pallas as pl
from jax.experimental.pallas import tpu as pltpu

HEAD_DIM = 128
MOBA_BLOCK = 256
MOBA_TOPK = 3
ROPE_THETA = 10000.0
CONV_WIDTH = 4
LRU_C = 8.0
TOP_K = 8
N_GROUPS = 8
TOPK_GROUPS = 4
ROUTED_SCALE = 2.5
LN_EPS = 1e-5
NEG = -1e30

VMEM_LIMIT_BYTES = 56 * 1024 * 1024
SUBLANES = 8
EXPERT_ROW_BLOCK = 256

F32 = jnp.float32
BF16 = jnp.bfloat16


def _params(*sem):
    return pltpu.CompilerParams(dimension_semantics=sem, vmem_limit_bytes=VMEM_LIMIT_BYTES)


def _tile(n, pref):
    t = min(n, pref)
    assert n % t == 0, (n, t)
    return t


def _proj_kernel(x_ref, w_ref, o_ref):
    o_ref[...] = jnp.dot(x_ref[...], w_ref[...], preferred_element_type=F32).astype(o_ref.dtype)


def _proj(x, w, out_dtype):
    m, k = x.shape
    n = w.shape[1]
    tm, tn = _tile(m, 1024), _tile(n, 1024)
    return pl.pallas_call(
        _proj_kernel,
        grid=(m // tm, n // tn),
        in_specs=[pl.BlockSpec((tm, k), lambda i, j: (i, 0)),
                  pl.BlockSpec((k, tn), lambda i, j: (0, j))],
        out_specs=pl.BlockSpec((tm, tn), lambda i, j: (i, j)),
        out_shape=jax.ShapeDtypeStruct((m, n), out_dtype),
        compiler_params=_params("parallel", "arbitrary"),
        name="proj",
    )(x, w)


def _rope_proj_kernel(x_ref, w_ref, cos_ref, sin_ref, o_ref, km_ref, *, n_rope_tiles, tm, tn):
    acc = jnp.dot(x_ref[...], w_ref[...], preferred_element_type=F32)
    j = pl.program_id(1)
    nb = tm // MOBA_BLOCK

    def block_means(val, sl):
        for r in range(nb):
            km_ref[r, :, sl] = jnp.mean(val[r * MOBA_BLOCK:(r + 1) * MOBA_BLOCK], axis=0, keepdims=True)

    @pl.when(j < n_rope_tiles)
    def _():
        cos = cos_ref[...]
        sin = sin_ref[...]
        for h in range(tn // HEAD_DIM):
            sl = slice(h * HEAD_DIM, (h + 1) * HEAD_DIM)
            seg = acc[:, sl]
            roped = seg * cos + pltpu.roll(seg, HEAD_DIM // 2, axis=1) * sin
            o_ref[:, sl] = roped.astype(o_ref.dtype)
            block_means(roped, sl)

    @pl.when(j >= n_rope_tiles)
    def _():
        o_ref[...] = acc.astype(o_ref.dtype)
        block_means(acc, slice(None))


def _rope_proj(x, w, cos, sin_signed, n_rope, seq):
    m, k = x.shape
    n = w.shape[1]
    tm = _tile(seq, 1024)
    tn = _tile(n_rope, 1024)
    assert n % tn == 0 and tm % MOBA_BLOCK == 0
    n_seq_tiles = seq // tm
    kern = functools.partial(_rope_proj_kernel, n_rope_tiles=n_rope // tn, tm=tm, tn=tn)
    return pl.pallas_call(
        kern,
        grid=(m // tm, n // tn),
        in_specs=[pl.BlockSpec((tm, k), lambda i, j: (i, 0)),
                  pl.BlockSpec((k, tn), lambda i, j: (0, j)),
                  pl.BlockSpec((tm, HEAD_DIM), lambda i, j: (i % n_seq_tiles, 0)),
                  pl.BlockSpec((tm, HEAD_DIM), lambda i, j: (i % n_seq_tiles, 0))],
        out_specs=[pl.BlockSpec((tm, tn), lambda i, j: (i, j)),
                   pl.BlockSpec((tm // MOBA_BLOCK, 1, tn), lambda i, j: (i, 0, j))],
        out_shape=[jax.ShapeDtypeStruct((m, n), BF16),
                   jax.ShapeDtypeStruct((m // MOBA_BLOCK, 1, n), F32)],
        compiler_params=_params("parallel", "arbitrary"),
        name="rope_proj",
    )(x, w, cos, sin_signed)


def _bf16_bits(v):
    return lax.bitcast_convert_type(v.astype(BF16).astype(F32), jnp.uint32)


def _pack_halves(v):
    half = v.shape[1] // 2
    return (_bf16_bits(v[:, :half]) >> 16) | _bf16_bits(v[:, half:])


def _unpack_halves(u):
    lo = lax.bitcast_convert_type(u << 16, F32).astype(BF16)
    hi = lax.bitcast_convert_type(u & jnp.uint32(0xFFFF0000), F32).astype(BF16)
    return jnp.concatenate([lo, hi], axis=1)


def _layer_norm_store(z, g_ref, b_ref, of_ref, ob_ref, op_ref=None):
    mu = jnp.mean(z, axis=-1, keepdims=True)
    zc = z - mu
    var = jnp.mean(zc * zc, axis=-1, keepdims=True)
    out = zc * lax.rsqrt(var + LN_EPS) * g_ref[...] + b_ref[...]
    of_ref[...] = out
    ob_ref[...] = out.astype(BF16)
    if op_ref is not None:
        op_ref[...] = _pack_halves(out)


def _mm_res_ln_kernel(y_ref, w_ref, x_ref, g_ref, b_ref, of_ref, ob_ref, op_ref, *, alpha):
    mix = jnp.dot(y_ref[...], w_ref[...], preferred_element_type=F32)
    _layer_norm_store(alpha * x_ref[...] + mix, g_ref, b_ref, of_ref, ob_ref, op_ref)


def _mm_res_ln(y, w, x, g, b, alpha):
    m, k = y.shape
    d = w.shape[1]
    tm = _tile(m, 256)
    row = lambda i: (i, 0)
    const = lambda i: (0, 0)
    return pl.pallas_call(
        functools.partial(_mm_res_ln_kernel, alpha=alpha),
        grid=(m // tm,),
        in_specs=[pl.BlockSpec((tm, k), row), pl.BlockSpec((k, d), const), pl.BlockSpec((tm, d), row),
                  pl.BlockSpec((1, d), const), pl.BlockSpec((1, d), const)],
        out_specs=[pl.BlockSpec((tm, d), row), pl.BlockSpec((tm, d), row), pl.BlockSpec((tm, d // 2), row)],
        out_shape=[jax.ShapeDtypeStruct((m, d), F32), jax.ShapeDtypeStruct((m, d), BF16),
                   jax.ShapeDtypeStruct((m, d // 2), jnp.uint32)],
        compiler_params=_params("parallel"),
        name="mm_res_ln",
    )(y, w, x, g.reshape(1, d), b.reshape(1, d))


def _rglru_kernel(gb_ref, u_ref, cw_ref, cb_ref, gw_ref, gbias_ref, sp_ref, y_ref,
                  uext_ref, h_ref, a_ref, b_ref, *, ts, n_blocks, cblk):
    t = pl.program_id(1)

    @pl.when(t == 0)
    def _():
        uext_ref[0:SUBLANES, :] = jnp.zeros((SUBLANES, uext_ref.shape[1]), F32)
        h_ref[...] = jnp.zeros_like(h_ref)

    uext_ref[SUBLANES:SUBLANES + ts, :] = u_ref[...]
    u = cb_ref[...] + sum(
        uext_ref[pl.ds(SUBLANES - (CONV_WIDTH - 1) + k, ts), :] * cw_ref[k:k + 1, :]
        for k in range(CONV_WIDTH))
    uext_ref[0:SUBLANES, :] = uext_ref[ts:ts + SUBLANES, :]

    for n in range(n_blocks):
        sl = slice(n * cblk, (n + 1) * cblk)
        un = u[:, sl]
        gates = jnp.dot(un.astype(BF16), gw_ref[n], preferred_element_type=F32) + gbias_ref[n]
        gates = jax.nn.sigmoid(gates)
        r = gates[:, :cblk]
        i_gate = gates[:, cblk:]
        log_a = -LRU_C * r * sp_ref[:, sl]
        a = jnp.exp(log_a)
        mult = jnp.sqrt(1.0 - jnp.exp(2.0 * log_a))
        a_ref[:, sl] = a
        b_ref[:, sl] = mult * i_gate * un

    width = a_ref.shape[1]
    row = lax.broadcasted_iota(jnp.int32, (SUBLANES, width), 0)

    def slab(s, h):
        off = pl.multiple_of(s * SUBLANES, SUBLANES)
        a = a_ref[pl.ds(off, SUBLANES), :]
        b = b_ref[pl.ds(off, SUBLANES), :]
        for sh in (1, 2, 4):
            keep = row >= sh
            a_prev = jnp.where(keep, pltpu.roll(a, sh, axis=0), 1.0)
            b_prev = jnp.where(keep, pltpu.roll(b, sh, axis=0), 0.0)
            b = a * b_prev + b
            a = a * a_prev
        hs = a * h + b
        b_ref[pl.ds(off, SUBLANES), :] = hs
        return hs[SUBLANES - 1:SUBLANES, :]

    h_ref[...] = lax.fori_loop(0, ts // SUBLANES, slab, h_ref[...])
    y_ref[...] = (jax.nn.gelu(gb_ref[...], approximate=True) * b_ref[...]).astype(y_ref.dtype)


def _rglru(proj, conv_w, conv_b, gate_w, gate_b, softplus_neg_lam, batch, seq):
    m, two_d = proj.shape
    d = two_d // 2
    n_blocks, cblk = gate_w.shape[0], gate_w.shape[1]
    ts = _tile(seq, 256)
    nt = seq // ts
    kern = functools.partial(_rglru_kernel, ts=ts, n_blocks=n_blocks, cblk=cblk)
    const2 = lambda b, t: (0, 0)
    const3 = lambda b, t: (0, 0, 0)
    return pl.pallas_call(
        kern,
        grid=(batch, nt),
        in_specs=[pl.BlockSpec((ts, d), lambda b, t: (b * nt + t, 0)),
                  pl.BlockSpec((ts, d), lambda b, t: (b * nt + t, 1)),
                  pl.BlockSpec((CONV_WIDTH, d), const2),
                  pl.BlockSpec((1, d), const2),
                  pl.BlockSpec((n_blocks, cblk, 2 * cblk), const3),
                  pl.BlockSpec((n_blocks, 1, 2 * cblk), const3),
                  pl.BlockSpec((1, d), const2)],
        out_specs=pl.BlockSpec((ts, d), lambda b, t: (b * nt + t, 0)),
        out_shape=jax.ShapeDtypeStruct((m, d), BF16),
        scratch_shapes=[pltpu.VMEM((ts + SUBLANES, d), F32), pltpu.VMEM((1, d), F32),
                        pltpu.VMEM((ts, d), F32), pltpu.VMEM((ts, d), F32)],
        compiler_params=_params("arbitrary", "arbitrary"),
        name="rglru",
    )(proj, proj, conv_w, conv_b.reshape(1, d), gate_w, gate_b.reshape(n_blocks, 1, 2 * cblk),
      softplus_neg_lam.reshape(1, d))


def _moba_kernel(q_ref, k_ref, vt_ref, km_ref, o_ref, acc_ref, *, heads, n_kv_blocks):
    i = pl.program_id(2)
    L = MOBA_BLOCK
    nt_dims = (((1,), (1,)), ((), ()))
    blk = lax.broadcasted_iota(jnp.int32, (n_kv_blocks, L), 0)
    key = lax.broadcasted_iota(jnp.int32, (L, L), 0)
    qry = lax.broadcasted_iota(jnp.int32, (L, L), 1)
    sls = [slice(h * HEAD_DIM, (h + 1) * HEAD_DIM) for h in range(heads)]
    qs = [q_ref[:, sl] for sl in sls]

    bits = []
    for h in range(heads):
        gate = lax.dot_general(km_ref[0, :, sls[h]], qs[h].astype(F32), nt_dims,
                               precision=lax.Precision.HIGHEST, preferred_element_type=F32)
        gate = jnp.where(blk < i, gate, -jnp.inf)
        chosen = jnp.zeros((n_kv_blocks, L), F32)
        for _ in range(MOBA_TOPK):
            best = jnp.max(gate, axis=0, keepdims=True)
            first = jnp.min(jnp.where(gate == best, blk, n_kv_blocks), axis=0, keepdims=True)
            pick = (blk == first) & (best > -jnp.inf)
            chosen = jnp.where(pick, 1.0, chosen)
            gate = jnp.where(pick, -jnp.inf, gate)
        weight = jnp.left_shift(1, blk).astype(F32)
        bits.append(jnp.sum(chosen * weight, axis=0, keepdims=True).astype(jnp.int32))
        acc_ref[h] = jnp.zeros((HEAD_DIM, L), F32)

    def scores(koff):
        return [lax.dot_general(k_ref[pl.ds(koff, L), sls[h]], qs[h], nt_dims, preferred_element_type=F32)
                for h in range(heads)]

    def accumulate(jblk, ps):
        for h in range(heads):
            acc_ref[h] += jnp.dot(vt_ref[0, jblk, sls[h], :], ps[h], preferred_element_type=F32)

    def past(j, carry):
        ss = scores(pl.multiple_of(j * L, L))
        ps, out = [], []
        for h in range(heads):
            m, l = carry[h]
            picked = (jnp.right_shift(bits[h], j) & 1) > 0
            m_new = jnp.where(picked, jnp.maximum(m, jnp.max(ss[h], axis=0, keepdims=True)), m)
            alpha = jnp.exp2(m - m_new)
            p = jnp.exp2(ss[h] - jnp.where(picked, m_new, -NEG))
            l = alpha * l + jnp.sum(p, axis=0, keepdims=True)
            acc_ref[h] = alpha * acc_ref[h]
            ps.append(p.astype(BF16))
            out.append((m_new, l))
        accumulate(j, ps)
        return tuple(out)

    def own(carry):
        ss = scores(pl.multiple_of(i * L, L))
        ps, ls = [], []
        for h in range(heads):
            m, l = carry[h]
            s = jnp.where(key <= qry, ss[h], NEG)
            m_new = jnp.maximum(m, jnp.max(s, axis=0, keepdims=True))
            alpha = jnp.exp2(m - m_new)
            p = jnp.exp2(s - m_new)
            ls.append(alpha * l + jnp.sum(p, axis=0, keepdims=True))
            acc_ref[h] = alpha * acc_ref[h]
            ps.append(p.astype(BF16))
        accumulate(i, ps)
        return ls

    init = tuple((jnp.full((1, L), NEG, F32), jnp.zeros((1, L), F32)) for _ in range(heads))
    ls = own(lax.fori_loop(0, i, past, init))
    for h in range(heads):
        o_ref[:, sls[h]] = (acc_ref[h] / ls[h]).T.astype(o_ref.dtype)


def _moba(q, kv, vt, k_mean, batch, seq):
    m, d_attn = q.shape
    heads = min(16, d_attn // HEAD_DIM)
    gw = heads * HEAD_DIM
    n_groups = d_attn // gw
    nq = seq // MOBA_BLOCK
    kern = functools.partial(_moba_kernel, heads=heads, n_kv_blocks=nq)
    return pl.pallas_call(
        kern,
        grid=(batch, n_groups, nq),
        in_specs=[pl.BlockSpec((MOBA_BLOCK, gw), lambda b, g, i: (b * nq + i, g)),
                  pl.BlockSpec((seq, gw), lambda b, g, i: (b, g), pipeline_mode=pl.Buffered(1)),
                  pl.BlockSpec((1, nq, gw, MOBA_BLOCK), lambda b, g, i: (b, 0, g, 0), pipeline_mode=pl.Buffered(1)),
                  pl.BlockSpec((1, nq, gw), lambda b, g, i: (b, 0, g))],
        out_specs=pl.BlockSpec((MOBA_BLOCK, gw), lambda b, g, i: (b * nq + i, g)),
        out_shape=jax.ShapeDtypeStruct((m, d_attn), BF16),
        scratch_shapes=[pltpu.VMEM((heads, HEAD_DIM, MOBA_BLOCK), F32)],
        compiler_params=_params("parallel", "parallel", "arbitrary"),
        name="moba",
    )(q, kv, vt, k_mean)


def _router_kernel(x_ref, wt_ref, bias_ref, tope_ref, w8_ref, rank8_ref, counts_ref, carry_ref,
                   *, tm, n_experts):
    @pl.when(pl.program_id(0) == 0)
    def _():
        carry_ref[...] = jnp.zeros_like(carry_ref)

    nt_dims = (((1,), (1,)), ((), ()))
    hi_lo = lambda v: (v.astype(BF16), (v - v.astype(BF16).astype(F32)).astype(BF16))
    xh, xl = hi_lo(x_ref[...])
    wh, wl = hi_lo(wt_ref[...])
    ntdot = lambda a, b: lax.dot_general(a, b, nt_dims, preferred_element_type=F32)
    logits = ntdot(wh, xh) + (ntdot(wl, xh) + ntdot(wh, xl))
    scores = jax.nn.sigmoid(logits)
    sel = scores + bias_ref[...]

    gsz = n_experts // N_GROUPS
    cmax = lambda v: jnp.max(v, axis=0, keepdims=True)
    csum = lambda v: jnp.sum(v, axis=0, keepdims=True)
    first_of = lambda v, m, idx, n: jnp.min(jnp.where(v == m, idx, float(n)), axis=0, keepdims=True)

    in_group = lax.broadcasted_iota(jnp.int32, (gsz, tm), 0).astype(F32)
    groups = [sel[g * gsz:(g + 1) * gsz, :] for g in range(N_GROUPS)]
    gscore = []
    for sg in groups:
        m1 = cmax(sg)
        m2 = cmax(jnp.where(in_group == first_of(sg, m1, in_group, gsz), -jnp.inf, sg))
        gscore.append(m1 + m2)
    gidx = lax.broadcasted_iota(jnp.int32, (N_GROUPS, tm), 0)
    gs = jnp.concatenate(gscore, axis=0)
    beaten = jnp.zeros((N_GROUPS, tm), jnp.int32)
    for o in range(N_GROUPS):
        ahead = (gscore[o] > gs) | ((gscore[o] == gs) & (o < gidx))
        beaten = beaten + ahead.astype(jnp.int32)
    keep = beaten < TOPK_GROUPS
    cand = jnp.concatenate([jnp.where(keep[g:g + 1, :], groups[g], -jnp.inf) for g in range(N_GROUPS)], axis=0)

    eidx = lax.broadcasted_iota(jnp.int32, (n_experts, tm), 0).astype(F32)
    chosen = jnp.zeros((n_experts, tm), F32)
    picks, tope_rows, score_rows = [], [], []
    for _ in range(TOP_K):
        idx = first_of(cand, cmax(cand), eidx, n_experts)
        hit = eidx == idx
        picks.append(hit)
        tope_rows.append(idx)
        score_rows.append(csum(jnp.where(hit, scores, 0.0)))
        chosen = jnp.where(hit, 1.0, chosen)
        cand = jnp.where(hit, -jnp.inf, cand)
    tope_ref[...] = jnp.concatenate(tope_rows, axis=0).astype(jnp.int32)
    s8 = jnp.concatenate(score_rows, axis=0)
    w8_ref[...] = s8 / csum(s8) * ROUTED_SCALE

    r = lax.broadcasted_iota(jnp.int32, (tm, tm), 0)
    c = lax.broadcasted_iota(jnp.int32, (tm, tm), 1)
    before = jnp.dot(chosen.astype(BF16), (r < c).astype(BF16), preferred_element_type=F32) + carry_ref[...]
    rank_rows = [csum(jnp.where(picks[k], before, 0.0)) for k in range(TOP_K)]
    rank8_ref[...] = jnp.concatenate(rank_rows, axis=0).astype(jnp.int32)
    carry_ref[...] += jnp.sum(chosen, axis=1, keepdims=True)
    counts_ref[...] = carry_ref[...].astype(jnp.int32)


def _router(x, w, bias):
    m, d = x.shape
    e = w.shape[1]
    tm = _tile(m, 256)
    const = lambda i: (0, 0)
    col = lambda i: (0, i)
    return pl.pallas_call(
        functools.partial(_router_kernel, tm=tm, n_experts=e),
        grid=(m // tm,),
        in_specs=[pl.BlockSpec((tm, d), lambda i: (i, 0)), pl.BlockSpec((e, d), const), pl.BlockSpec((e, 1), const)],
        out_specs=[pl.BlockSpec((TOP_K, tm), col), pl.BlockSpec((TOP_K, tm), col), pl.BlockSpec((TOP_K, tm), col),
                   pl.BlockSpec((e, 1), const)],
        out_shape=[jax.ShapeDtypeStruct((TOP_K, m), jnp.int32), jax.ShapeDtypeStruct((TOP_K, m), F32),
                   jax.ShapeDtypeStruct((TOP_K, m), jnp.int32), jax.ShapeDtypeStruct((e, 1), jnp.int32)],
        scratch_shapes=[pltpu.VMEM((e, 1), F32)],
        compiler_params=_params("arbitrary"),
        name="router",
    )(x, w.astype(F32).T, bias.astype(F32).reshape(e, 1))


def _dispatch_kernel(dest_ref, x_ref, init_ref, o_ref, sem, *, tm):
    del init_ref

    def send(r, carry):
        for k in range(TOP_K):
            d = dest_ref[0, 0, r * TOP_K + k]
            pltpu.make_async_copy(x_ref.at[pl.ds(r, 1), :], o_ref.at[pl.ds(d, 1), :], sem).start(priority=k % 2)
        return carry

    lax.fori_loop(0, tm, send, 0)
    for _ in range(TOP_K):
        pltpu.make_async_copy(x_ref, o_ref.at[pl.ds(0, tm), :], sem).wait()


def _dispatch(x, dest, init):
    n, d = x.shape
    tm = _tile(n, 256)
    assert init.shape[1] == d and init.dtype == x.dtype
    return pl.pallas_call(
        functools.partial(_dispatch_kernel, tm=tm),
        grid=(n // tm,),
        in_specs=[pl.BlockSpec((1, 1, tm * TOP_K), lambda i: (i, 0, 0), memory_space=pltpu.SMEM),
                  pl.BlockSpec((tm, d), lambda i: (i, 0)),
                  pl.BlockSpec(memory_space=pl.ANY)],
        out_specs=pl.BlockSpec(memory_space=pl.ANY),
        out_shape=jax.ShapeDtypeStruct(init.shape, x.dtype),
        scratch_shapes=[pltpu.SemaphoreType.DMA(())],
        input_output_aliases={2: 0},
        compiler_params=_params("arbitrary"),
        name="dispatch",
    )(dest.reshape(n // tm, 1, tm * TOP_K), x, init)


def _experts_kernel(be_ref, nxt_ref, slot_ref, nv_ref, x_ref, wgu_hbm, wdn_hbm, o_ref,
                    wgu_f32, wdn_f32, wgu_bf, wdn_bf, sems, *, d_expert, layer):
    i = pl.program_id(0)
    e = be_ref[i]
    slot = slot_ref[i]
    live = i < nv_ref[0]

    def weight_copies(expert, s):
        return (pltpu.make_async_copy(wgu_hbm.at[layer, expert], wgu_f32.at[s], sems.at[0, s]),
                pltpu.make_async_copy(wdn_hbm.at[layer, expert], wdn_f32.at[s], sems.at[1, s]))

    @pl.when(live & (i == 0))
    def _():
        for cp in weight_copies(e, slot):
            cp.start(priority=1)

    @pl.when(live & ((i == 0) | (e != be_ref[jnp.maximum(i - 1, 0)])))
    def _():
        for cp in weight_copies(e, slot):
            cp.wait()
        nxt = nxt_ref[i]

        @pl.when(nxt >= 0)
        def _():
            for cp in weight_copies(nxt, 1 - slot):
                cp.start(priority=1)

        wgu_bf[...] = wgu_f32[slot].astype(BF16)
        wdn_bf[...] = wdn_f32[slot].astype(BF16)

    @pl.when(live)
    def _():
        h = jnp.dot(_unpack_halves(x_ref[...]), wgu_bf[...], preferred_element_type=F32)
        act = jax.nn.silu(h[:, :d_expert]) * h[:, d_expert:]
        o_ref[...] = jnp.dot(act.astype(BF16), wdn_bf[...], preferred_element_type=F32).astype(o_ref.dtype)

    @pl.when(jnp.logical_not(live))
    def _():
        o_ref[...] = jnp.zeros_like(o_ref)


def _experts(x_rows, block_e, next_e, slot, n_valid, w_gu, w_down, layer):
    p, half = x_rows.shape
    d = w_gu.shape[2]
    assert d == 2 * half and x_rows.dtype == jnp.uint32
    d_expert = w_down.shape[2]
    tm = EXPERT_ROW_BLOCK
    grid_spec = pltpu.PrefetchScalarGridSpec(
        num_scalar_prefetch=4,
        grid=(p // tm,),
        in_specs=[pl.BlockSpec((tm, half), lambda i, be, nx, sl, nv: (jnp.maximum(jnp.minimum(i, nv[0] - 1), 0), 0)),
                  pl.BlockSpec(memory_space=pl.ANY),
                  pl.BlockSpec(memory_space=pl.ANY)],
        out_specs=pl.BlockSpec((tm, d), lambda i, be, nx, sl, nv: (i, 0)),
        scratch_shapes=[pltpu.VMEM((2, d, 2 * d_expert), F32), pltpu.VMEM((2, d_expert, d), F32),
                        pltpu.VMEM((d, 2 * d_expert), BF16), pltpu.VMEM((d_expert, d), BF16),
                        pltpu.SemaphoreType.DMA((2, 2))],
    )
    return pl.pallas_call(
        functools.partial(_experts_kernel, d_expert=d_expert, layer=layer),
        grid_spec=grid_spec,
        out_shape=jax.ShapeDtypeStruct((p, d), BF16),
        compiler_params=_params("arbitrary"),
        name="experts",
    )(block_e, next_e, slot, n_valid, x_rows, w_gu, w_down)


def _shared_ln_kernel(xb_ref, x_ref, yg_ref, w8_ref, wgu_ref, wdn_ref, g_ref, b_ref, of_ref, ob_ref,
                      *, alpha, d_expert):
    h = jnp.dot(xb_ref[...], wgu_ref[...], preferred_element_type=F32)
    act = jax.nn.silu(h[:, :d_expert]) * h[:, d_expert:]
    ffn = jnp.dot(act.astype(BF16), wdn_ref[...], preferred_element_type=F32)
    w8 = w8_ref[...]
    for k in range(TOP_K):
        ffn = ffn + w8[:, k:k + 1] * yg_ref[k].astype(F32)
    _layer_norm_store(alpha * x_ref[...] + ffn, g_ref, b_ref, of_ref, ob_ref)


def _shared_ln(xb, x, yg, w8, w_gu, w_down, g, b, alpha):
    m, d = x.shape
    d_expert = w_down.shape[0]
    tm = _tile(m, 256)
    row = lambda i: (i, 0)
    const = lambda i: (0, 0)
    return pl.pallas_call(
        functools.partial(_shared_ln_kernel, alpha=alpha, d_expert=d_expert),
        grid=(m // tm,),
        in_specs=[pl.BlockSpec((tm, d), row), pl.BlockSpec((tm, d), row),
                  pl.BlockSpec((TOP_K, tm, d), lambda i: (0, i, 0)),
                  pl.BlockSpec((tm, TOP_K), row),
                  pl.BlockSpec((d, 2 * d_expert), const), pl.BlockSpec((d_expert, d), const),
                  pl.BlockSpec((1, d), const), pl.BlockSpec((1, d), const)],
        out_specs=[pl.BlockSpec((tm, d), row), pl.BlockSpec((tm, d), row)],
        out_shape=[jax.ShapeDtypeStruct((m, d), F32), jax.ShapeDtypeStruct((m, d), BF16)],
        compiler_params=_params("parallel"),
        name="shared_ln",
    )(xb, x, yg, w8, w_gu, w_down, g.reshape(1, d), b.reshape(1, d))


def _moe(x, xb, xp, router_w, router_bias, w_gu, w_down, layer, sh_gu, sh_down, g, b, alpha, rows_buf):
    n, d = x.shape
    n_experts = router_w.shape[1]
    tm = EXPERT_ROW_BLOCK
    top_e, w8, rank8, counts = _router(x, router_w, router_bias)

    counts = counts[:, 0]
    experts = jnp.arange(n_experts, dtype=jnp.int32)
    padded = (counts + tm - 1) // tm * tm
    pend = jnp.cumsum(padded)
    pstart = pend - padded
    dest = rank8 + jnp.sum(jnp.where(top_e[..., None] == experts, pstart, 0), axis=-1)
    n_blocks = (n * TOP_K + n_experts * (tm - 1)) // tm + 1
    n_valid = (pend[-1] // tm).astype(jnp.int32)
    blk_start = jnp.minimum(jnp.arange(n_blocks, dtype=jnp.int32), n_valid - 1) * tm
    block_e = jnp.minimum(jnp.sum((pend[None, :] <= blk_start[:, None]).astype(jnp.int32), axis=1), n_experts - 1)
    nonempty = counts > 0
    later = (experts[None, :] > block_e[:, None]) & nonempty[None, :]
    next_e = jnp.min(jnp.where(later, experts[None, :], n_experts), axis=1)
    next_e = jnp.where(next_e == n_experts, -1, next_e).astype(jnp.int32)
    earlier = (experts[None, :] < block_e[:, None]) & nonempty[None, :]
    slot = (jnp.sum(earlier.astype(jnp.int32), axis=1) % 2).astype(jnp.int32)

    if rows_buf is None:
        rows_buf = jnp.zeros((n_blocks * tm, xp.shape[1]), xp.dtype)
    x_rows = _dispatch(xp, dest.T, rows_buf)
    y_rows = _experts(x_rows, block_e, next_e, slot, n_valid.reshape(1), w_gu, w_down, layer)
    yg = y_rows[dest.reshape(-1)].reshape(TOP_K, n, d)
    xf, xb = _shared_ln(xb, x, yg, w8.T, sh_gu, sh_down, g, b, alpha)
    return xf, xb, x_rows


def kernel(x, ln_mix_g, ln_mix_b, ln_ffn_g, ln_ffn_b, a_w_in, a_conv_w, a_conv_b, a_gate_w, a_gate_b, a_lambda, a_w_out, kv_w, b_w_q, b_w_o, router_w, router_bias, moe_w_gu, moe_w_down, sh_w_gu, sh_w_down):
    batch, seq, d = x.shape
    n = batch * seq
    n_a = a_w_in.shape[0]
    n_b = b_w_q.shape[0]
    depth = n_a + n_b
    alpha = (2.0 * depth) ** 0.25
    d_attn = b_w_q.shape[2]

    inv = ROPE_THETA ** (-jnp.arange(0, HEAD_DIM, 2, dtype=F32) / HEAD_DIM)
    ang = jnp.arange(seq, dtype=F32)[:, None] * inv[None, :]
    ang = jnp.concatenate([ang, ang], axis=-1)
    cos = jnp.cos(ang)
    sign = jnp.where(jnp.arange(HEAD_DIM) < HEAD_DIM // 2, -1.0, 1.0).astype(F32)
    sin_signed = jnp.sin(ang) * sign
    q_scale = HEAD_DIM ** -0.5 * math.log2(math.e)

    xf = x.reshape(n, d)
    xb = xf.astype(BF16)
    kv = k_mean = v_t = rows_buf = None
    for layer in range(depth):
        if layer < n_a:
            proj = _proj(xb, a_w_in[layer].astype(BF16), F32)
            y = _rglru(proj, a_conv_w[layer], a_conv_b[layer], a_gate_w[layer].astype(BF16), a_gate_b[layer],
                       jax.nn.softplus(-a_lambda[layer].astype(F32)), batch, seq)
            w_o = a_w_out[layer]
        else:
            j = layer - n_a
            if j == 0:
                kv, k_mean = _rope_proj(xb, kv_w.astype(BF16), cos, sin_signed, d_attn, seq)
                k_mean = k_mean.reshape(batch, seq // MOBA_BLOCK, 2 * d_attn)
                v_t = jnp.swapaxes(kv[:, d_attn:].reshape(batch, seq // MOBA_BLOCK, MOBA_BLOCK, d_attn), 2, 3)
            q, _ = _rope_proj(xb, b_w_q[j].astype(BF16), cos * q_scale, sin_signed * q_scale, d_attn, seq)
            y = _moba(q, kv, v_t, k_mean, batch, seq)
            w_o = b_w_o[j]
        xf, xb, xp = _mm_res_ln(y, w_o.astype(BF16), xf, ln_mix_g[layer], ln_mix_b[layer], alpha)
        xf, xb, rows_buf = _moe(xf, xb, xp, router_w[layer], router_bias[layer], moe_w_gu, moe_w_down, layer,
                                sh_w_gu[layer].astype(BF16), sh_w_down[layer].astype(BF16),
                                ln_ffn_g[layer], ln_ffn_b[layer], alpha, rows_buf)
    return xf.reshape(batch, seq, d)
```
